```python
import jax, jax.numpy as jnp
from jax import lax
import numpy as np

D_MODEL = 1024
BATCH = 1
SEQ = 16384
DEPTH = 4

N_MIXERS = 2
N_MLA = (DEPTH + 1) // 2
N_CONV = DEPTH // 2
N_HEADS = 16
QK_NOPE = 64
QK_ROPE = 32
V_HEAD = 64
Q_LORA = 256
KV_LORA = 128
ROPE_THETA = 10000.0
Q_BLOCK = 128
CONV_W = 3
N_EXPERTS = 16
N_GROUPS = 4
EXPERTS_PER_GROUP = N_EXPERTS // N_GROUPS
TOP_K = 2
D_EXPERT = 512
DEEPNORM_ALPHA = float((2 * DEPTH) ** 0.25)
DEEPNORM_BETA = float((8 * DEPTH) ** -0.25)
LN_EPS = 1e-5
RMS_EPS = 1e-6

kernel_name = "hybrid_mla_shortconv_grouped_moe_deepnorm"


def layer_norm(x, g, b):
    xf = x.astype(jnp.float32)
    mu = jnp.mean(xf, axis=-1, keepdims=True)
    var = jnp.mean(jnp.square(xf - mu), axis=-1, keepdims=True)
    y = (xf - mu) * lax.rsqrt(var + LN_EPS) * g.astype(jnp.float32) + b.astype(jnp.float32)
    return y.astype(x.dtype)


def rms_norm(x, g):
    xf = x.astype(jnp.float32)
    y = xf * lax.rsqrt(jnp.mean(jnp.square(xf), axis=-1, keepdims=True) + RMS_EPS) * g.astype(jnp.float32)
    return y.astype(x.dtype)


def rope_tables(positions, dtype):
    inv_freq = ROPE_THETA ** (-jnp.arange(0, QK_ROPE, 2, dtype=jnp.float32) / QK_ROPE)
    ang = positions.astype(jnp.float32)[..., None] * inv_freq
    return jnp.cos(ang).astype(dtype), jnp.sin(ang).astype(dtype)


def apply_rope(t, cos, sin):
    t1, t2 = jnp.split(t, 2, axis=-1)
    return jnp.concatenate([t1 * cos - t2 * sin, t2 * cos + t1 * sin], axis=-1)


def mla(x, cos, sin, w_dqkv, q_norm, kv_norm, w_uq, w_ukv, w_o):
    b, s, _ = x.shape
    lat = x @ w_dqkv
    c_q, c_kv, k_rope = jnp.split(lat, [Q_LORA, Q_LORA + KV_LORA], axis=-1)
    c_q = rms_norm(c_q, q_norm)
    c_kv = rms_norm(c_kv, kv_norm)
    q = (c_q @ w_uq).reshape(b, s, N_HEADS, QK_NOPE + QK_ROPE)
    q_nope = q[..., :QK_NOPE]
    q_rope = apply_rope(q[..., QK_NOPE:], cos[:, :, None], sin[:, :, None])
    kv = (c_kv @ w_ukv).reshape(b, s, N_HEADS, QK_NOPE + V_HEAD)
    k_nope, v = kv[..., :QK_NOPE], kv[..., QK_NOPE:]
    k_rope = apply_rope(k_rope, cos, sin)
    scale = (QK_NOPE + QK_ROPE) ** -0.5
    nb = s // Q_BLOCK
    key_idx = jnp.arange(s, dtype=jnp.int32)

    def to_blocks(t):
        return jnp.moveaxis(t.reshape(b, nb, Q_BLOCK, *t.shape[2:]), 1, 0)

    def attend_block(args):
        qn, qr, start = args
        scores = (jnp.einsum('bqhd,bkhd->bhqk', qn, k_nope, preferred_element_type=jnp.float32)
                  + jnp.einsum('bqhr,bkr->bhqk', qr, k_rope, preferred_element_type=jnp.float32)) * scale
        q_idx = start + jnp.arange(Q_BLOCK, dtype=jnp.int32)
        causal = key_idx[None, :] <= q_idx[:, None]
        scores = jnp.where(causal, scores, -jnp.inf)
        p = jax.nn.softmax(scores, axis=-1).astype(v.dtype)
        return jnp.einsum('bhqk,bkhv->bqhv', p, v)

    starts = jnp.arange(nb, dtype=jnp.int32) * Q_BLOCK
    out = lax.map(attend_block, (to_blocks(q_nope), to_blocks(q_rope), starts))
    out = jnp.moveaxis(out, 0, 1).reshape(b, s, N_HEADS * V_HEAD)
    return out @ w_o


def short_conv(x, w_in, conv_k, w_out):
    s = x.shape[1]
    b_gate, c_gate, u = jnp.split(x @ w_in, 3, axis=-1)
    u = c_gate * u
    u_pad = jnp.pad(u, ((0, 0), (CONV_W - 1, 0), (0, 0)))
    conv = sum(u_pad[:, j:j + s] * conv_k[j] for j in range(CONV_W))
    return (b_gate * conv) @ w_out


def route(x, router_w, router_bias):
    scores = jax.nn.sigmoid((x @ router_w).astype(jnp.float32))
    biased = scores + router_bias.astype(jnp.float32)
    grouped = biased.reshape(*biased.shape[:-1], N_GROUPS, EXPERTS_PER_GROUP)
    group_score = jnp.sum(lax.top_k(grouped, TOP_K)[0], axis=-1)
    g = jnp.argmax(group_score, axis=-1)
    in_group = jnp.take_along_axis(grouped, g[..., None, None], axis=-2)[..., 0, :]
    _, local = lax.top_k(in_group, TOP_K)
    idx = g[..., None] * EXPERTS_PER_GROUP + local
    w = jnp.take_along_axis(scores, idx, axis=-1)
    w = w / jnp.sum(w, axis=-1, keepdims=True)
    combine = jnp.sum(jax.nn.one_hot(idx, N_EXPERTS, dtype=jnp.float32) * w[..., None], axis=-2)
    return combine.astype(x.dtype)


def moe(x, combine, w_gate, w_up, w_down):
    def expert_step(acc, params):
        wg, wu, wd, c = params
        h = jax.nn.silu(x @ wg) * (x @ wu)
        return acc + (h @ wd) * c[..., None], None

    y, _ = lax.scan(expert_step, jnp.zeros_like(x),
                    (w_gate, w_up, w_down, jnp.moveaxis(combine, -1, 0)))
    return y


def setup_inputs(seed: int = 0) -> dict:
    key = jax.random.key(seed)
    ks = jax.random.split(key, 24)
    f32 = jnp.float32

    def nrm(k, shape, scale):
        return jax.random.normal(k, shape, f32) * scale

    x = jax.random.normal(ks[0], (BATCH, SEQ, D_MODEL), f32)
    positions = jnp.broadcast_to(jnp.arange(SEQ, dtype=jnp.int32)[None], (BATCH, SEQ))
    return {
        "x": x,
        "positions": positions,
        "mla_w_dqkv": nrm(ks[1], (N_MLA, D_MODEL, Q_LORA + KV_LORA + QK_ROPE), D_MODEL ** -0.5),
        "mla_q_norm": 1.0 + nrm(ks[2], (N_MLA, Q_LORA), 0.02),
        "mla_kv_norm": 1.0 + nrm(ks[3], (N_MLA, KV_LORA), 0.02),
        "mla_w_uq": nrm(ks[4], (N_MLA, Q_LORA, N_HEADS * (QK_NOPE + QK_ROPE)), Q_LORA ** -0.5),
        "mla_w_ukv": nrm(ks[5], (N_MLA, KV_LORA, N_HEADS * (QK_NOPE + V_HEAD)), KV_LORA ** -0.5),
        "mla_w_o": nrm(ks[6], (N_MLA, N_HEADS * V_HEAD, D_MODEL), (N_HEADS * V_HEAD) ** -0.5 * DEEPNORM_BETA),
        "conv_w_in": nrm(ks[7], (N_CONV, D_MODEL, 3 * D_MODEL), D_MODEL ** -0.5),
        "conv_kernel": nrm(ks[8], (N_CONV, CONV_W, D_MODEL), CONV_W ** -0.5),
        "conv_w_out": nrm(ks[9], (N_CONV, D_MODEL, D_MODEL), D_MODEL ** -0.5 * DEEPNORM_BETA),
        "router_w": nrm(ks[10], (D_MODEL, N_EXPERTS), D_MODEL ** -0.5),
        "router_bias": nrm(ks[11], (N_EXPERTS,), 0.01),
        "moe_w_gate": nrm(ks[12], (DEPTH, N_EXPERTS, D_MODEL, D_EXPERT), D_MODEL ** -0.5),
        "moe_w_up": nrm(ks[13], (DEPTH, N_EXPERTS, D_MODEL, D_EXPERT), D_MODEL ** -0.5),
        "moe_w_down": nrm(ks[14], (DEPTH, N_EXPERTS, D_EXPERT, D_MODEL), D_EXPERT ** -0.5 * DEEPNORM_BETA),
        "ln_mix_g": 1.0 + nrm(ks[15], (DEPTH, D_MODEL), 0.02),
        "ln_mix_b": nrm(ks[16], (DEPTH, D_MODEL), 0.02),
        "ln_ffn_g": 1.0 + nrm(ks[17], (DEPTH, D_MODEL), 0.02),
        "ln_ffn_b": nrm(ks[18], (DEPTH, D_MODEL), 0.02),
    }


def reference(x, positions, mla_w_dqkv, mla_q_norm, mla_kv_norm, mla_w_uq, mla_w_ukv, mla_w_o,
              conv_w_in, conv_kernel, conv_w_out, router_w, router_bias,
              moe_w_gate, moe_w_up, moe_w_down, ln_mix_g, ln_mix_b, ln_ffn_g, ln_ffn_b):
    cos, sin = rope_tables(positions, x.dtype)
    for i in range(DEPTH):
        j = i // N_MIXERS
        if i % N_MIXERS == 0:
            h = mla(x, cos, sin, mla_w_dqkv[j], mla_q_norm[j], mla_kv_norm[j],
                    mla_w_uq[j], mla_w_ukv[j], mla_w_o[j])
        else:
            h = short_conv(x, conv_w_in[j], conv_kernel[j], conv_w_out[j])
        x = layer_norm(DEEPNORM_ALPHA * x + h, ln_mix_g[i], ln_mix_b[i])
        combine = route(x, router_w, router_bias)
        h = moe(x, combine, moe_w_gate[i], moe_w_up[i], moe_w_down[i])
        x = layer_norm(DEEPNORM_ALPHA * x + h, ln_ffn_g[i], ln_ffn_b[i])
    return x
```

```python
import functools

import jax
import jax.numpy as jnp
from jax import lax
from jax.experimental import pallas as pl
from jax.experimental.pallas import tpu as pltpu

D_MODEL = 1024
DEPTH = 4
N_HEADS = 16
QK_NOPE = 64
QK_ROPE = 32
V_HEAD = 64
Q_LORA = 256
KV_LORA = 128
ROPE_THETA = 10000.0
CONV_W = 3
N_EXPERTS = 16
N_GROUPS = 4
EXPERTS_PER_GROUP = N_EXPERTS // N_GROUPS
D_EXPERT = 512
DEEPNORM_ALPHA = float((2 * DEPTH) ** 0.25)
LN_EPS = 1e-5
RMS_EPS = 1e-6
ATTN_SCALE = (QK_NOPE + QK_ROPE) ** -0.5

LANES = 128
HALF_ROPE = QK_ROPE // 2
LAT_W = Q_LORA + KV_LORA + 2 * LANES
ROUTE_ROWS = 8
VMEM_LIMIT = 56 * 1024 * 1024

TOKEN_TILE = 512
ATTN_TILE = 512
MOE_TILE = 512

F32 = jnp.float32
BF16 = jnp.bfloat16


def _params(*sem):
    return pltpu.CompilerParams(dimension_semantics=sem, vmem_limit_bytes=VMEM_LIMIT)


def _const_spec(shape):
    return pl.BlockSpec(shape, lambda *_: (0,) * len(shape))


def _dot(a, b):
    return jnp.dot(a, b, preferred_element_type=F32)


def _layer_norm(z, g, b):
    mu = jnp.mean(z, axis=-1, keepdims=True)
    zc = z - mu
    var = jnp.mean(zc * zc, axis=-1, keepdims=True)
    return zc * lax.rsqrt(var + LN_EPS) * g + b


def _rms_norm(z, g):
    return z * lax.rsqrt(jnp.mean(z * z, axis=-1, keepdims=True) + RMS_EPS) * g


def _mla_proj_kernel(x_ref, pos_ref, invf_ref, wd_ref, qn_ref, kvn_ref, wq_ref, wkv_ref,
                     q_out, k_out, v_out):
    xb = x_ref[...].astype(BF16)
    lat = _dot(xb, wd_ref[...])
    cq = _rms_norm(lat[:, :Q_LORA], qn_ref[...]).astype(BF16)
    ckv = _rms_norm(lat[:, Q_LORA:Q_LORA + KV_LORA], kvn_ref[...]).astype(BF16)
    kr = lat[:, Q_LORA + KV_LORA:Q_LORA + KV_LORA + LANES]
    krr = lat[:, Q_LORA + KV_LORA + LANES:]
    ang = pos_ref[...].astype(F32) * invf_ref[...]
    cos = jnp.cos(ang)
    sin = jnp.sin(ang)
    k_rope = kr * cos + krr * sin
    for h in range(N_HEADS):
        qq = _dot(cq, wq_ref[:, h * 2 * LANES:(h + 1) * 2 * LANES])
        q_out[h] = ((qq[:, :LANES] * cos + qq[:, LANES:] * sin) * ATTN_SCALE).astype(BF16)
        kv = _dot(ckv, wkv_ref[:, h * 2 * LANES:(h + 1) * 2 * LANES])
        k_out[h] = (kv[:, :LANES] + k_rope).astype(BF16)
        v_out[h] = kv[:, LANES:].astype(BF16)


def _mla_proj(x, pos_col, invf, wd, qn, kvn, wq, wkv):
    s = x.shape[0]
    ts = min(TOKEN_TILE, s)
    head_spec = pl.BlockSpec((N_HEADS, ts, LANES), lambda i: (0, i, 0))
    out_sds = jax.ShapeDtypeStruct((N_HEADS, s, LANES), BF16)
    return pl.pallas_call(
        _mla_proj_kernel,
        grid=(s // ts,),
        in_specs=[
            pl.BlockSpec((ts, D_MODEL), lambda i: (i, 0)),
            pl.BlockSpec((ts, 1), lambda i: (i, 0)),
            _const_spec((1, LANES)),
            _const_spec((D_MODEL, LAT_W)),
            _const_spec((1, Q_LORA)),
            _const_spec((1, KV_LORA)),
            _const_spec((Q_LORA, N_HEADS * 2 * LANES)),
            _const_spec((KV_LORA, N_HEADS * 2 * LANES)),
        ],
        out_specs=[head_spec, head_spec, head_spec],
        out_shape=[out_sds, out_sds, out_sds],
        compiler_params=_params("parallel"),
        name="mla_proj",
    )(x, pos_col, invf, wd, qn, kvn, wq, wkv)


def _attn_kernel(q_ref, k_ref, v_ref, o_ref, m_sc, l_sc, acc_sc, *, tile):
    qi = pl.program_id(1)
    q = q_ref[0]
    m_sc[...] = jnp.full(m_sc.shape, -jnp.inf, F32)
    l_sc[...] = jnp.zeros(l_sc.shape, F32)
    acc_sc[...] = jnp.zeros(acc_sc.shape, F32)

    def step(j, masked):
        start = pl.multiple_of(j * tile, tile)
        k = k_ref[0, pl.ds(start, tile), :]
        v = v_ref[0, pl.ds(start, tile), :]
        s = lax.dot_general(q, k, (((1,), (1,)), ((), ())), preferred_element_type=F32)
        if masked:
            row = lax.broadcasted_iota(jnp.int32, s.shape, 0)
            col = lax.broadcasted_iota(jnp.int32, s.shape, 1)
            s = jnp.where(col <= row, s, -jnp.inf)
        m_prev = m_sc[...]
        m_new = jnp.maximum(m_prev, jnp.max(s, axis=-1, keepdims=True))
        alpha = jnp.exp(m_prev - m_new)
        p = jnp.exp(s - m_new)
        l_sc[...] = alpha * l_sc[...] + jnp.sum(p, axis=-1, keepdims=True)
        acc_sc[...] = alpha * acc_sc[...] + _dot(p.astype(BF16), v)
        m_sc[...] = m_new

    def body(j, carry):
        step(j, False)
        return carry

    lax.fori_loop(0, qi, body, 0)
    step(qi, True)
    o_ref[...] = (acc_sc[...] / l_sc[...]).astype(o_ref.dtype)


def _attention(q, k, v):
    s = q.shape[1]
    tile = min(ATTN_TILE, s)
    kv_spec = pl.BlockSpec((1, s, LANES), lambda h, i: (h, 0, 0))
    return pl.pallas_call(
        functools.partial(_attn_kernel, tile=tile),
        grid=(N_HEADS, s // tile),
        in_specs=[pl.BlockSpec((1, tile, LANES), lambda h, i: (h, i, 0)), kv_spec, kv_spec],
        out_specs=pl.BlockSpec((tile, LANES), lambda h, i: (i, h)),
        out_shape=jax.ShapeDtypeStruct((s, N_HEADS * LANES), BF16),
        scratch_shapes=[
            pltpu.VMEM((tile, 1), F32),
            pltpu.VMEM((tile, 1), F32),
            pltpu.VMEM((tile, LANES), F32),
        ],
        compiler_params=_params("parallel", "parallel"),
        name="mla_attention",
    )(q, k, v)


def _top2_of_4(a0, a1, a2, a3):
    hi1, lo1 = jnp.maximum(a0, a1), jnp.minimum(a0, a1)
    hi2, lo2 = jnp.maximum(a2, a3), jnp.minimum(a2, a3)
    first = jnp.maximum(hi1, hi2)
    second = jnp.maximum(jnp.minimum(hi1, hi2), jnp.maximum(lo1, lo2))
    return first + second


def _argmax_first(vals):
    best, idx = vals[0], jnp.zeros_like(vals[0])
    for i in range(1, len(vals)):
        take = vals[i] > best
        best = jnp.where(take, vals[i], best)
        idx = jnp.where(take, float(i), idx)
    return idx, best


def _pick(idx, rows):
    out = rows[-1]
    for i in range(len(rows) - 2, -1, -1):
        out = jnp.where(idx == float(i), rows[i], out)
    return out


def _route_rows(y, rw_ref, rbias_ref):
    y_hi = y.astype(BF16)
    y_lo = (y - y_hi.astype(F32)).astype(BF16)
    prod = _dot(y_hi, rw_ref[...]) + _dot(y_lo, rw_ref[...])
    prod_t = prod.T
    logits = prod_t[:N_EXPERTS] + prod_t[N_EXPERTS:2 * N_EXPERTS]
    scores = 1.0 / (1.0 + jnp.exp(-logits))
    biased = scores + rbias_ref[...]
    b = [biased[e:e + 1] for e in range(N_EXPERTS)]
    sc = [scores[e:e + 1] for e in range(N_EXPERTS)]
    group_scores = [_top2_of_4(*b[g * 4:g * 4 + 4]) for g in range(N_GROUPS)]
    g_idx, _ = _argmax_first(group_scores)
    in_b = [_pick(g_idx, [b[g * 4 + j] for g in range(N_GROUPS)]) for j in range(EXPERTS_PER_GROUP)]
    in_s = [_pick(g_idx, [sc[g * 4 + j] for g in range(N_GROUPS)]) for j in range(EXPERTS_PER_GROUP)]
    l1, _ = _argmax_first(in_b)
    rest = [jnp.where(l1 == float(j), -jnp.inf, in_b[j]) for j in range(EXPERTS_PER_GROUP)]
    l2, _ = _argmax_first(rest)
    w1 = _pick(l1, in_s)
    w2 = _pick(l2, in_s)
    wsum = w1 + w2
    e1 = g_idx * float(EXPERTS_PER_GROUP) + l1
    e2 = g_idx * float(EXPERTS_PER_GROUP) + l2
    zero = jnp.zeros_like(e1)
    return jnp.concatenate([e1, e2, w1 / wsum, w2 / wsum] + [zero] * (ROUTE_ROWS - 4), axis=0)


def _mixer_epilogue(x, h, g_ref, b_ref, rw_ref, rbias_ref, y_out, yb_out, rt_out):
    y = _layer_norm(DEEPNORM_ALPHA * x + h, g_ref[...], b_ref[...])
    y_out[...] = y
    yb_out[...] = y.astype(BF16)
    rt_out[...] = _route_rows(y, rw_ref, rbias_ref)


def _epilogue_specs(ts):
    in_specs = [
        _const_spec((1, D_MODEL)),
        _const_spec((1, D_MODEL)),
        _const_spec((D_MODEL, LANES)),
        _const_spec((N_EXPERTS, 1)),
    ]
    out_specs = [
        pl.BlockSpec((ts, D_MODEL), lambda i: (i, 0)),
        pl.BlockSpec((ts, D_MODEL), lambda i: (i, 0)),
        pl.BlockSpec((ROUTE_ROWS, ts), lambda i: (0, i)),
    ]
    return in_specs, out_specs


def _epilogue_shapes(s):
    return [
        jax.ShapeDtypeStruct((s, D_MODEL), F32),
        jax.ShapeDtypeStruct((s, D_MODEL), BF16),
        jax.ShapeDtypeStruct((ROUTE_ROWS, s), F32),
    ]


def _attn_out_kernel(x_ref, o_ref, wo_ref, g_ref, b_ref, rw_ref, rbias_ref, y_out, yb_out, rt_out):
    h = _dot(o_ref[...], wo_ref[...])
    _mixer_epilogue(x_ref[...], h, g_ref, b_ref, rw_ref, rbias_ref, y_out, yb_out, rt_out)


def _attn_out(x, o, wo, g, b, rw, rbias):
    s = x.shape[0]
    ts = min(TOKEN_TILE, s)
    ep_in, ep_out = _epilogue_specs(ts)
    return pl.pallas_call(
        _attn_out_kernel,
        grid=(s // ts,),
        in_specs=[
            pl.BlockSpec((ts, D_MODEL), lambda i: (i, 0)),
            pl.BlockSpec((ts, N_HEADS * LANES), lambda i: (i, 0)),
            _const_spec((N_HEADS * LANES, D_MODEL)),
        ] + ep_in,
        out_specs=ep_out,
        out_shape=_epilogue_shapes(s),
        compiler_params=_params("parallel"),
        name="attn_out_ln_route",
    )(x, o, wo, g, b, rw, rbias)


def _conv_kernel(x_ref, win_ref, ck_ref, wout_ref, g_ref, b_ref, rw_ref, rbias_ref,
                 y_out, yb_out, rt_out, u_sc, *, ts):
    @pl.when(pl.program_id(0) == 0)
    def _():
        u_sc[0:8, :] = jnp.zeros((8, D_MODEL), F32)

    x = x_ref[...]
    xb = x.astype(BF16)
    c_gate = _dot(xb, win_ref[:, D_MODEL:2 * D_MODEL])
    u = c_gate * _dot(xb, win_ref[:, 2 * D_MODEL:])
    u_sc[8:8 + ts, :] = u
    ck = ck_ref[...]
    conv = u_sc[6:6 + ts, :] * ck[0:1] + u_sc[7:7 + ts, :] * ck[1:2] + u * ck[2:3]
    u_sc[0:8, :] = u_sc[ts:ts + 8, :]
    b_gate = _dot(xb, win_ref[:, :D_MODEL])
    h = _dot((b_gate * conv).astype(BF16), wout_ref[...])
    _mixer_epilogue(x, h, g_ref, b_ref, rw_ref, rbias_ref, y_out, yb_out, rt_out)


def _conv_mixer(x, win, ck, wout, g, b, rw, rbias):
    s = x.shape[0]
    ts = min(TOKEN_TILE, s)
    ep_in, ep_out = _epilogue_specs(ts)
    return pl.pallas_call(
        functools.partial(_conv_kernel, ts=ts),
        grid=(s // ts,),
        in_specs=[
            pl.BlockSpec((ts, D_MODEL), lambda i: (i, 0)),
            _const_spec((D_MODEL, 3 * D_MODEL)),
            _const_spec((CONV_W, D_MODEL)),
            _const_spec((D_MODEL, D_MODEL)),
        ] + ep_in,
        out_specs=ep_out,
        out_shape=_epilogue_shapes(s),
        scratch_shapes=[pltpu.VMEM((ts + 8, D_MODEL), F32)],
        compiler_params=_params("arbitrary"),
        name="conv_mixer_ln_route",
    )(x, win, ck, wout, g, b, rw, rbias)


def _moe_kernel(yb_ref, y_ref, rt_ref, wg_ref, wu_ref, wd_ref, g_ref, b_ref, out_ref, acc_sc):
    e = pl.program_id(1)

    @pl.when(e == 0)
    def _():
        acc_sc[...] = jnp.zeros(acc_sc.shape, F32)

    xb = yb_ref[...]
    hg = _dot(xb, wg_ref[0])
    hu = _dot(xb, wu_ref[0])
    hidden = (hg * (1.0 / (1.0 + jnp.exp(-hg)))) * hu
    yo = _dot(hidden.astype(BF16), wd_ref[0])
    rt = rt_ref[...]
    ef = e.astype(F32)
    c = jnp.where(rt[:, 0:1] == ef, rt[:, 2:3], 0.0) + jnp.where(rt[:, 1:2] == ef, rt[:, 3:4], 0.0)
    acc_sc[...] += yo * c

    @pl.when(e == N_EXPERTS - 1)
    def _():
        out_ref[...] = _layer_norm(DEEPNORM_ALPHA * y_ref[...] + acc_sc[...], g_ref[...], b_ref[...])


def _moe(yb, y, rt_cols, wg, wu, wd, g, b):
    s = y.shape[0]
    tm = min(MOE_TILE, s)
    tok = lambda i, e: (i, 0)
    return pl.pallas_call(
        _moe_kernel,
        grid=(s // tm, N_EXPERTS),
        in_specs=[
            pl.BlockSpec((tm, D_MODEL), tok),
            pl.BlockSpec((tm, D_MODEL), tok),
            pl.BlockSpec((tm, ROUTE_ROWS), tok),
            pl.BlockSpec((1, D_MODEL, D_EXPERT), lambda i, e: (e, 0, 0)),
            pl.BlockSpec((1, D_MODEL, D_EXPERT), lambda i, e: (e, 0, 0)),
            pl.BlockSpec((1, D_EXPERT, D_MODEL), lambda i, e: (e, 0, 0)),
            _const_spec((1, D_MODEL)),
            _const_spec((1, D_MODEL)),
        ],
        out_specs=pl.BlockSpec((tm, D_MODEL), tok),
        out_shape=jax.ShapeDtypeStruct((s, D_MODEL), F32),
        scratch_shapes=[pltpu.VMEM((tm, D_MODEL), F32)],
        compiler_params=_params("parallel", "arbitrary"),
        name="moe_ln",
    )(yb, y, rt_cols, wg, wu, wd, g, b)


def _rope_lane_pad(t):
    pad = [(0, 0)] * (t.ndim - 1)
    return jnp.pad(t, pad + [(QK_NOPE, LANES - QK_NOPE - QK_ROPE)])


def _rotate_half(t):
    return jnp.concatenate([-t[..., HALF_ROPE:], t[..., :HALF_ROPE]], axis=-1)


def _prep_mla(w_dqkv, w_uq, w_ukv, w_o):
    w_kr = w_dqkv[:, Q_LORA + KV_LORA:]
    wd = jnp.concatenate(
        [w_dqkv[:, :Q_LORA + KV_LORA], _rope_lane_pad(w_kr), _rope_lane_pad(_rotate_half(w_kr))],
        axis=1).astype(BF16)
    uq = w_uq.reshape(Q_LORA, N_HEADS, QK_NOPE + QK_ROPE)
    zq = jnp.zeros((Q_LORA, N_HEADS, LANES - QK_NOPE - QK_ROPE), F32)
    q_plain = jnp.concatenate([uq, zq], axis=-1)
    q_rot = jnp.concatenate(
        [jnp.zeros((Q_LORA, N_HEADS, QK_NOPE), F32), _rotate_half(uq[..., QK_NOPE:]), zq], axis=-1)
    wq = jnp.concatenate([q_plain, q_rot], axis=-1).reshape(Q_LORA, N_HEADS * 2 * LANES).astype(BF16)
    ukv = w_ukv.reshape(KV_LORA, N_HEADS, QK_NOPE + V_HEAD)
    zk = jnp.zeros((KV_LORA, N_HEADS, LANES - QK_NOPE), F32)
    zv = jnp.zeros((KV_LORA, N_HEADS, LANES - V_HEAD), F32)
    wkv = jnp.concatenate([ukv[..., :QK_NOPE], zk, ukv[..., QK_NOPE:], zv], axis=-1)
    wkv = wkv.reshape(KV_LORA, N_HEADS * 2 * LANES).astype(BF16)
    wo = jnp.pad(w_o.reshape(N_HEADS, V_HEAD, D_MODEL), ((0, 0), (0, LANES - V_HEAD), (0, 0)))
    wo = wo.reshape(N_HEADS * LANES, D_MODEL).astype(BF16)
    return wd, wq, wkv, wo


def _prep_router(router_w, router_bias):
    hi = router_w.astype(BF16)
    lo = (router_w - hi.astype(F32)).astype(BF16)
    rw = jnp.concatenate([hi, lo, jnp.zeros((D_MODEL, LANES - 2 * N_EXPERTS), BF16)], axis=1)
    return rw, router_bias.reshape(N_EXPERTS, 1).astype(F32)


def kernel(x, positions, mla_w_dqkv, mla_q_norm, mla_kv_norm, mla_w_uq, mla_w_ukv, mla_w_o,
           conv_w_in, conv_kernel, conv_w_out, router_w, router_bias,
           moe_w_gate, moe_w_up, moe_w_down, ln_mix_g, ln_mix_b, ln_ffn_g, ln_ffn_b):
    batch, s, _ = x.shape
    rw, rbias = _prep_router(router_w, router_bias)
    inv_freq = ROPE_THETA ** (-jnp.arange(0, QK_ROPE, 2, dtype=F32) / QK_ROPE)
    invf = _rope_lane_pad(jnp.concatenate([inv_freq, inv_freq]))[None, :]
    wg = moe_w_gate.astype(BF16)
    wu = moe_w_up.astype(BF16)
    wdn = moe_w_down.astype(BF16)
    row = lambda t: t.reshape(1, -1).astype(F32)

    outs = []
    for bi in range(batch):
        xc = x[bi]
        pos_col = positions[bi].reshape(s, 1)
        for i in range(DEPTH):
            j = i // 2
            if i % 2 == 0:
                wd, wq, wkv, wo = _prep_mla(mla_w_dqkv[j], mla_w_uq[j], mla_w_ukv[j], mla_w_o[j])
                q, k, v = _mla_proj(xc, pos_col, invf, wd, row(mla_q_norm[j]), row(mla_kv_norm[j]),
                                    wq, wkv)
                o = _attention(q, k, v)
                y, yb, rt = _attn_out(xc, o, wo, row(ln_mix_g[i]), row(ln_mix_b[i]), rw, rbias)
            else:
                y, yb, rt = _conv_mixer(xc, conv_w_in[j].astype(BF16), conv_kernel[j].astype(F32),
                                        conv_w_out[j].astype(BF16), row(ln_mix_g[i]),
                                        row(ln_mix_b[i]), rw, rbias)
            xc = _moe(yb, y, rt.T, wg[i], wu[i], wdn[i], row(ln_ffn_g[i]), row(ln_ffn_b[i]))
        outs.append(xc)
    return jnp.stack(outs, axis=0)
```

```python
import functools

import jax
import jax.numpy as jnp
from jax import lax
from jax.experimental import pallas as pl
from jax.experimental.pallas import tpu as pltpu

D_MODEL = 1024
DEPTH = 4
N_HEADS = 16
QK_NOPE = 64
QK_ROPE = 32
V_HEAD = 64
Q_LORA = 256
KV_LORA = 128
ROPE_THETA = 10000.0
CONV_W = 3
N_EXPERTS = 16
N_GROUPS = 4
EXPERTS_PER_GROUP = N_EXPERTS // N_GROUPS
D_EXPERT = 512
DEEPNORM_ALPHA = float((2 * DEPTH) ** 0.25)
LN_EPS = 1e-5
RMS_EPS = 1e-6
QK_SCALE = (QK_NOPE + QK_ROPE) ** -0.5 * 1.4426950408889634

LANES = 128
SUBLANES = 8
HALF_ROPE = QK_ROPE // 2
LAT_W = Q_LORA + KV_LORA + 2 * LANES
ROUTE_ROWS = 8
VMEM_LIMIT = 56 * 1024 * 1024

TOKEN_TILE = 512
ATTN_TILE = 512
MOE_TILE = 512

F32 = jnp.float32
BF16 = jnp.bfloat16
_NT = (((1,), (1,)), ((), ()))


def _params(*sem):
    return pltpu.CompilerParams(dimension_semantics=sem, vmem_limit_bytes=VMEM_LIMIT)


def _const_spec(shape):
    return pl.BlockSpec(shape, lambda *_: (0,) * len(shape))


def _dot(a, b):
    return jnp.dot(a, b, preferred_element_type=F32)


def _layer_norm(z, g, b):
    mu = jnp.mean(z, axis=-1, keepdims=True)
    zc = z - mu
    var = jnp.mean(zc * zc, axis=-1, keepdims=True)
    return zc * lax.rsqrt(var + LN_EPS) * g + b


def _rms_norm(z, g):
    return z * lax.rsqrt(jnp.mean(z * z, axis=-1, keepdims=True) + RMS_EPS) * g


def _mla_proj_kernel(x_ref, pos_ref, invf_ref, wd_ref, qn_ref, kvn_ref, wq_ref, wk_ref, wvt_ref,
                     q_out, k_out, vt_out):
    xb = x_ref[...].astype(BF16)
    lat = _dot(xb, wd_ref[...])
    cq = _rms_norm(lat[:, :Q_LORA], qn_ref[...]).astype(BF16)
    ckv = _rms_norm(lat[:, Q_LORA:Q_LORA + KV_LORA], kvn_ref[...]).astype(BF16)
    kr = lat[:, Q_LORA + KV_LORA:Q_LORA + KV_LORA + LANES]
    krr = lat[:, Q_LORA + KV_LORA + LANES:]
    ang = pos_ref[...].astype(F32) * invf_ref[...]
    cos = jnp.cos(ang)
    sin = jnp.sin(ang)
    k_rope = kr * cos + krr * sin
    ones_row = lax.broadcasted_iota(jnp.int32, (LANES, x_ref.shape[0]), 0) == V_HEAD
    for h in range(N_HEADS):
        qq = _dot(cq, wq_ref[:, h * 2 * LANES:(h + 1) * 2 * LANES])
        q_out[h] = ((qq[:, :LANES] * cos + qq[:, LANES:] * sin) * QK_SCALE).astype(BF16)
        k_out[h] = (_dot(ckv, wk_ref[:, h * LANES:(h + 1) * LANES]) + k_rope).astype(BF16)
        vt = lax.dot_general(wvt_ref[h * LANES:(h + 1) * LANES, :], ckv, _NT, preferred_element_type=F32)
        vt_out[h, 0] = jnp.where(ones_row, 1.0, vt).astype(BF16)


def _mla_proj(x, pos_col, invf, wd, qn, kvn, wq, wk, wvt):
    s = x.shape[0]
    ts = min(ATTN_TILE, s)
    head_spec = pl.BlockSpec((N_HEADS, ts, LANES), lambda i: (0, i, 0))
    out_sds = jax.ShapeDtypeStruct((N_HEADS, s, LANES), BF16)
    return pl.pallas_call(
        _mla_proj_kernel,
        grid=(s // ts,),
        in_specs=[
            pl.BlockSpec((ts, D_MODEL), lambda i: (i, 0)),
            pl.BlockSpec((ts, 1), lambda i: (i, 0)),
            _const_spec((1, LANES)),
            _const_spec((D_MODEL, LAT_W)),
            _const_spec((1, Q_LORA)),
            _const_spec((1, KV_LORA)),
            _const_spec((Q_LORA, N_HEADS * 2 * LANES)),
            _const_spec((KV_LORA, N_HEADS * LANES)),
            _const_spec((N_HEADS * LANES, KV_LORA)),
        ],
        out_specs=[head_spec, head_spec,
                   pl.BlockSpec((N_HEADS, 1, LANES, ts), lambda i: (0, i, 0, 0))],
        out_shape=[out_sds, out_sds,
                   jax.ShapeDtypeStruct((N_HEADS, s // ts, LANES, ts), BF16)],
        compiler_params=_params("parallel"),
        name="mla_proj",
    )(x, pos_col, invf, wd, qn, kvn, wq, wk, wvt)


def _sublane_allmax(part):
    for shift in (4, 2, 1):
        part = jnp.maximum(part, pltpu.roll(part, shift, 0))
    return part


def _attn_kernel(q_ref, k_ref, vt_ref, o_ref, s_a, s_b, mx_a, mx_b, acc_sc, m_sc, *, t):
    qi = pl.program_id(1)
    q = q_ref[0]
    groups = t // SUBLANES
    m_sc[...] = jnp.full(m_sc.shape, -jnp.inf, F32)
    acc_sc[...] = jnp.zeros(acc_sc.shape, F32)

    def scores(j, s_dst, mx_dst, diagonal):
        k = k_ref[0, pl.ds(pl.multiple_of(j * t, t), t), :]
        s = lax.dot_general(k, q, _NT, preferred_element_type=F32)
        if diagonal:
            key = lax.broadcasted_iota(jnp.int32, s.shape, 0)
            qry = lax.broadcasted_iota(jnp.int32, s.shape, 1)
            s = jnp.where(key <= qry, s, -jnp.inf)
        s_dst[...] = s
        mx_dst[...] = jnp.max(s.reshape(groups, SUBLANES, t), axis=0)

    def softmax_pv(j, s_src, mx_src):
        m_prev = m_sc[...]
        m_new = jnp.maximum(m_prev, _sublane_allmax(mx_src[...]))
        alpha = jnp.exp2(m_prev - m_new)
        p = jnp.exp2(s_src[...].reshape(groups, SUBLANES, t) - m_new[None])
        pv = _dot(vt_ref[0, j], p.reshape(t, t).astype(BF16))
        acc = acc_sc[...].reshape(LANES // SUBLANES, SUBLANES, t) * alpha[None]
        acc_sc[...] = acc.reshape(LANES, t) + pv
        m_sc[...] = m_new

    scores(0, s_a, mx_a, False)

    @pl.when(qi == 0)
    def _():
        scores(0, s_a, mx_a, True)

    def pair(i, carry):
        j = 2 * i
        scores(j + 1, s_b, mx_b, False)
        softmax_pv(j, s_a, mx_a)
        scores(j + 2, s_a, mx_a, False)
        softmax_pv(j + 1, s_b, mx_b)
        return carry

    n_pairs = jnp.maximum((qi - 1) // 2, 0)
    lax.fori_loop(0, n_pairs, pair, 0)
    j0 = 2 * n_pairs

    @pl.when(jnp.logical_and(qi > 0, qi % 2 == 1))
    def _():
        scores(j0 + 1, s_b, mx_b, True)
        softmax_pv(j0, s_a, mx_a)
        softmax_pv(j0 + 1, s_b, mx_b)

    @pl.when(jnp.logical_and(qi > 0, qi % 2 == 0))
    def _():
        scores(j0 + 1, s_b, mx_b, False)
        softmax_pv(j0, s_a, mx_a)
        scores(j0 + 2, s_a, mx_a, True)
        softmax_pv(j0 + 1, s_b, mx_b)
        softmax_pv(j0 + 2, s_a, mx_a)

    @pl.when(qi == 0)
    def _():
        softmax_pv(0, s_a, mx_a)

    acc = acc_sc[...]
    o_t = acc / acc[V_HEAD:V_HEAD + 1, :]
    o_ref[...] = o_t.T.astype(o_ref.dtype)


def _attention(q, k, vt):
    s = q.shape[1]
    t = vt.shape[-1]
    return pl.pallas_call(
        functools.partial(_attn_kernel, t=t),
        grid=(N_HEADS, s // t),
        in_specs=[
            pl.BlockSpec((1, t, LANES), lambda h, i: (h, i, 0)),
            pl.BlockSpec((1, s, LANES), lambda h, i: (h, 0, 0)),
            pl.BlockSpec((1, s // t, LANES, t), lambda h, i: (h, 0, 0, 0)),
        ],
        out_specs=pl.BlockSpec((t, LANES), lambda h, i: (i, h)),
        out_shape=jax.ShapeDtypeStruct((s, N_HEADS * LANES), BF16),
        scratch_shapes=[
            pltpu.VMEM((t, t), F32),
            pltpu.VMEM((t, t), F32),
            pltpu.VMEM((SUBLANES, t), F32),
            pltpu.VMEM((SUBLANES, t), F32),
            pltpu.VMEM((LANES, t), F32),
            pltpu.VMEM((SUBLANES, t), F32),
        ],
        compiler_params=_params("parallel", "parallel"),
        name="mla_attention",
    )(q, k, vt)


def _top2_of_4(a0, a1, a2, a3):
    hi1, lo1 = jnp.maximum(a0, a1), jnp.minimum(a0, a1)
    hi2, lo2 = jnp.maximum(a2, a3), jnp.minimum(a2, a3)
    first = jnp.maximum(hi1, hi2)
    second = jnp.maximum(jnp.minimum(hi1, hi2), jnp.maximum(lo1, lo2))
    return first + second


def _argmax_first(vals):
    best, idx = vals[0], jnp.zeros_like(vals[0])
    for i in range(1, len(vals)):
        take = vals[i] > best
        best = jnp.where(take, vals[i], best)
        idx = jnp.where(take, float(i), idx)
    return idx, best


def _pick(idx, rows):
    out = rows[-1]
    for i in range(len(rows) - 2, -1, -1):
        out = jnp.where(idx == float(i), rows[i], out)
    return out


def _route_rows(y, rw_ref, rbias_ref):
    y_hi = y.astype(BF16)
    y_lo = (y - y_hi.astype(F32)).astype(BF16)
    prod = _dot(y_hi, rw_ref[...]) + _dot(y_lo, rw_ref[...])
    prod_t = prod.T
    logits = prod_t[:N_EXPERTS] + prod_t[N_EXPERTS:2 * N_EXPERTS]
    scores = 1.0 / (1.0 + jnp.exp(-logits))
    biased = scores + rbias_ref[...]
    b = [biased[e:e + 1] for e in range(N_EXPERTS)]
    sc = [scores[e:e + 1] for e in range(N_EXPERTS)]
    group_scores = [_top2_of_4(*b[g * 4:g * 4 + 4]) for g in range(N_GROUPS)]
    g_idx, _ = _argmax_first(group_scores)
    in_b = [_pick(g_idx, [b[g * 4 + j] for g in range(N_GROUPS)]) for j in range(EXPERTS_PER_GROUP)]
    in_s = [_pick(g_idx, [sc[g * 4 + j] for g in range(N_GROUPS)]) for j in range(EXPERTS_PER_GROUP)]
    l1, _ = _argmax_first(in_b)
    rest = [jnp.where(l1 == float(j), -jnp.inf, in_b[j]) for j in range(EXPERTS_PER_GROUP)]
    l2, _ = _argmax_first(rest)
    w1 = _pick(l1, in_s)
    w2 = _pick(l2, in_s)
    wsum = w1 + w2
    e1 = g_idx * float(EXPERTS_PER_GROUP) + l1
    e2 = g_idx * float(EXPERTS_PER_GROUP) + l2
    zero = jnp.zeros_like(e1)
    return jnp.concatenate([e1, e2, w1 / wsum, w2 / wsum] + [zero] * (ROUTE_ROWS - 4), axis=0)


def _mixer_epilogue(x, h, g_ref, b_ref, rw_ref, rbias_ref, y_out, yb_out, rt_out):
    y = _layer_norm(DEEPNORM_ALPHA * x + h, g_ref[...], b_ref[...])
    y_out[...] = y
    yb_out[...] = y.astype(BF16)
    rt_out[...] = _route_rows(y, rw_ref, rbias_ref)


def _epilogue_specs(ts):
    in_specs = [
        _const_spec((1, D_MODEL)),
        _const_spec((1, D_MODEL)),
        _const_spec((D_MODEL, LANES)),
        _const_spec((N_EXPERTS, 1)),
    ]
    out_specs = [
        pl.BlockSpec((ts, D_MODEL), lambda i: (i, 0)),
        pl.BlockSpec((ts, D_MODEL), lambda i: (i, 0)),
        pl.BlockSpec((ROUTE_ROWS, ts), lambda i: (0, i)),
    ]
    return in_specs, out_specs


def _epilogue_shapes(s):
    return [
        jax.ShapeDtypeStruct((s, D_MODEL), F32),
        jax.ShapeDtypeStruct((s, D_MODEL), BF16),
        jax.ShapeDtypeStruct((ROUTE_ROWS, s), F32),
    ]


def _attn_out_kernel(x_ref, o_ref, wo_ref, g_ref, b_ref, rw_ref, rbias_ref, y_out, yb_out, rt_out):
    h = _dot(o_ref[...], wo_ref[...])
    _mixer_epilogue(x_ref[...], h, g_ref, b_ref, rw_ref, rbias_ref, y_out, yb_out, rt_out)


def _attn_out(x, o, wo, g, b, rw, rbias):
    s = x.shape[0]
    ts = min(TOKEN_TILE, s)
    ep_in, ep_out = _epilogue_specs(ts)
    return pl.pallas_call(
        _attn_out_kernel,
        grid=(s // ts,),
        in_specs=[
            pl.BlockSpec((ts, D_MODEL), lambda i: (i, 0)),
            pl.BlockSpec((ts, N_HEADS * LANES), lambda i: (i, 0)),
            _const_spec((N_HEADS * LANES, D_MODEL)),
        ] + ep_in,
        out_specs=ep_out,
        out_shape=_epilogue_shapes(s),
        compiler_params=_params("parallel"),
        name="attn_out_ln_route",
    )(x, o, wo, g, b, rw, rbias)


def _conv_kernel(x_ref, win_ref, ck_ref, wout_ref, g_ref, b_ref, rw_ref, rbias_ref,
                 y_out, yb_out, rt_out, u_sc, *, ts):
    @pl.when(pl.program_id(0) == 0)
    def _():
        u_sc[0:8, :] = jnp.zeros((8, D_MODEL), F32)

    x = x_ref[...]
    xb = x.astype(BF16)
    c_gate = _dot(xb, win_ref[:, D_MODEL:2 * D_MODEL])
    u = c_gate * _dot(xb, win_ref[:, 2 * D_MODEL:])
    u_sc[8:8 + ts, :] = u
    ck = ck_ref[...]
    conv = u_sc[6:6 + ts, :] * ck[0:1] + u_sc[7:7 + ts, :] * ck[1:2] + u * ck[2:3]
    u_sc[0:8, :] = u_sc[ts:ts + 8, :]
    b_gate = _dot(xb, win_ref[:, :D_MODEL])
    h = _dot((b_gate * conv).astype(BF16), wout_ref[...])
    _mixer_epilogue(x, h, g_ref, b_ref, rw_ref, rbias_ref, y_out, yb_out, rt_out)


def _conv_mixer(x, win, ck, wout, g, b, rw, rbias):
    s = x.shape[0]
    ts = min(TOKEN_TILE, s)
    ep_in, ep_out = _epilogue_specs(ts)
    return pl.pallas_call(
        functools.partial(_conv_kernel, ts=ts),
        grid=(s // ts,),
        in_specs=[
            pl.BlockSpec((ts, D_MODEL), lambda i: (i, 0)),
            _const_spec((D_MODEL, 3 * D_MODEL)),
            _const_spec((CONV_W, D_MODEL)),
            _const_spec((D_MODEL, D_MODEL)),
        ] + ep_in,
        out_specs=ep_out,
        out_shape=_epilogue_shapes(s),
        scratch_shapes=[pltpu.VMEM((ts + 8, D_MODEL), F32)],
        compiler_params=_params("arbitrary"),
        name="conv_mixer_ln_route",
    )(x, win, ck, wout, g, b, rw, rbias)


def _moe_kernel(yb_ref, y_ref, rt_ref, wg_ref, wu_ref, wd_ref, g_ref, b_ref, out_ref, acc_sc):
    e = pl.program_id(1)

    @pl.when(e == 0)
    def _():
        acc_sc[...] = jnp.zeros(acc_sc.shape, F32)

    xb = yb_ref[...]
    hg = _dot(xb, wg_ref[0])
    hu = _dot(xb, wu_ref[0])
    hidden = (hg * (1.0 / (1.0 + jnp.exp(-hg)))) * hu
    yo = _dot(hidden.astype(BF16), wd_ref[0])
    rt = rt_ref[...]
    ef = e.astype(F32)
    c = jnp.where(rt[:, 0:1] == ef, rt[:, 2:3], 0.0) + jnp.where(rt[:, 1:2] == ef, rt[:, 3:4], 0.0)
    acc_sc[...] += yo * c

    @pl.when(e == N_EXPERTS - 1)
    def _():
        out_ref[...] = _layer_norm(DEEPNORM_ALPHA * y_ref[...] + acc_sc[...], g_ref[...], b_ref[...])


def _moe(yb, y, rt_cols, wg, wu, wd, g, b):
    s = y.shape[0]
    tm = min(MOE_TILE, s)
    tok = lambda i, e: (i, 0)
    return pl.pallas_call(
        _moe_kernel,
        grid=(s // tm, N_EXPERTS),
        in_specs=[
            pl.BlockSpec((tm, D_MODEL), tok),
            pl.BlockSpec((tm, D_MODEL), tok),
            pl.BlockSpec((tm, ROUTE_ROWS), tok),
            pl.BlockSpec((1, D_MODEL, D_EXPERT), lambda i, e: (e, 0, 0)),
            pl.BlockSpec((1, D_MODEL, D_EXPERT), lambda i, e: (e, 0, 0)),
            pl.BlockSpec((1, D_EXPERT, D_MODEL), lambda i, e: (e, 0, 0)),
            _const_spec((1, D_MODEL)),
            _const_spec((1, D_MODEL)),
        ],
        out_specs=pl.BlockSpec((tm, D_MODEL), tok),
        out_shape=jax.ShapeDtypeStruct((s, D_MODEL), F32),
        scratch_shapes=[pltpu.VMEM((tm, D_MODEL), F32)],
        compiler_params=_params("parallel", "arbitrary"),
        name="moe_ln",
    )(yb, y, rt_cols, wg, wu, wd, g, b)


def _rope_lane_pad(t):
    pad = [(0, 0)] * (t.ndim - 1)
    return jnp.pad(t, pad + [(QK_NOPE, LANES - QK_NOPE - QK_ROPE)])


def _rotate_half(t):
    return jnp.concatenate([-t[..., HALF_ROPE:], t[..., :HALF_ROPE]], axis=-1)


def _prep_mla(w_dqkv, w_uq, w_ukv, w_o):
    w_kr = w_dqkv[:, Q_LORA + KV_LORA:]
    wd = jnp.concatenate(
        [w_dqkv[:, :Q_LORA + KV_LORA], _rope_lane_pad(w_kr), _rope_lane_pad(_rotate_half(w_kr))],
        axis=1).astype(BF16)
    uq = w_uq.reshape(Q_LORA, N_HEADS, QK_NOPE + QK_ROPE)
    zq = jnp.zeros((Q_LORA, N_HEADS, LANES - QK_NOPE - QK_ROPE), F32)
    q_plain = jnp.concatenate([uq, zq], axis=-1)
    q_rot = jnp.concatenate(
        [jnp.zeros((Q_LORA, N_HEADS, QK_NOPE), F32), _rotate_half(uq[..., QK_NOPE:]), zq], axis=-1)
    wq = jnp.concatenate([q_plain, q_rot], axis=-1).reshape(Q_LORA, N_HEADS * 2 * LANES).astype(BF16)
    ukv = w_ukv.reshape(KV_LORA, N_HEADS, QK_NOPE + V_HEAD)
    wk = jnp.pad(ukv[..., :QK_NOPE], ((0, 0), (0, 0), (0, LANES - QK_NOPE)))
    wk = wk.reshape(KV_LORA, N_HEADS * LANES).astype(BF16)
    wvt = jnp.pad(ukv[..., QK_NOPE:], ((0, 0), (0, 0), (0, LANES - V_HEAD)))
    wvt = wvt.reshape(KV_LORA, N_HEADS * LANES).T.astype(BF16)
    wo = jnp.pad(w_o.reshape(N_HEADS, V_HEAD, D_MODEL), ((0, 0), (0, LANES - V_HEAD), (0, 0)))
    wo = wo.reshape(N_HEADS * LANES, D_MODEL).astype(BF16)
    return wd, wq, wk, wvt, wo


def _prep_router(router_w, router_bias):
    hi = router_w.astype(BF16)
    lo = (router_w - hi.astype(F32)).astype(BF16)
    rw = jnp.concatenate([hi, lo, jnp.zeros((D_MODEL, LANES - 2 * N_EXPERTS), BF16)], axis=1)
    return rw, router_bias.reshape(N_EXPERTS, 1).astype(F32)


def kernel(x, positions, mla_w_dqkv, mla_q_norm, mla_kv_norm, mla_w_uq, mla_w_ukv, mla_w_o,
           conv_w_in, conv_kernel, conv_w_out, router_w, router_bias,
           moe_w_gate, moe_w_up, moe_w_down, ln_mix_g, ln_mix_b, ln_ffn_g, ln_ffn_b):
    batch, s, _ = x.shape
    rw, rbias = _prep_router(router_w, router_bias)
    inv_freq = ROPE_THETA ** (-jnp.arange(0, QK_ROPE, 2, dtype=F32) / QK_ROPE)
    invf = _rope_lane_pad(jnp.concatenate([inv_freq, inv_freq]))[None, :]
    wg = moe_w_gate.astype(BF16)
    wu = moe_w_up.astype(BF16)
    wdn = moe_w_down.astype(BF16)
    row = lambda t: t.reshape(1, -1).astype(F32)

    outs = []
    for bi in range(batch):
        xc = x[bi]
        pos_col = positions[bi].reshape(s, 1)
        for i in range(DEPTH):
            j = i // 2
            if i % 2 == 0:
                wd, wq, wk, wvt, wo = _prep_mla(mla_w_dqkv[j], mla_w_uq[j], mla_w_ukv[j], mla_w_o[j])
                q, k, vt = _mla_proj(xc, pos_col, invf, wd, row(mla_q_norm[j]), row(mla_kv_norm[j]),
                                     wq, wk, wvt)
                o = _attention(q, k, vt)
                y, yb, rt = _attn_out(xc, o, wo, row(ln_mix_g[i]), row(ln_mix_b[i]), rw, rbias)
            else:
                y, yb, rt = _conv_mixer(xc, conv_w_in[j].astype(BF16), conv_kernel[j].astype(F32),
                                        conv_w_out[j].astype(BF16), row(ln_mix_g[i]),
                                        row(ln_mix_b[i]), rw, rbias)
            xc = _moe(yb, y, rt.T, wg[i], wu[i], wdn[i], row(ln_ffn_g[i]), row(ln_ffn_b[i]))
        outs.append(xc)
    return jnp.stack(outs, axis=0)
```

```python
import functools

import jax
import jax.numpy as jnp
from jax import lax
from jax.experimental import pallas as pl
from jax.experimental.pallas import tpu as pltpu

D_MODEL = 1024
DEPTH = 4
N_HEADS = 16
QK_NOPE = 64
QK_ROPE = 32
V_HEAD = 64
Q_LORA = 256
KV_LORA = 128
ROPE_THETA = 10000.0
CONV_W = 3
N_EXPERTS = 16
N_GROUPS = 4
EXPERTS_PER_GROUP = N_EXPERTS // N_GROUPS
D_EXPERT = 512
DEEPNORM_ALPHA = float((2 * DEPTH) ** 0.25)
LN_EPS = 1e-5
RMS_EPS = 1e-6
QK_SCALE = (QK_NOPE + QK_ROPE) ** -0.5 * 1.4426950408889634

LANES = 128
SUBLANES = 8
VT_ROWS = 80
HALF_ROPE = QK_ROPE // 2
LAT_W = Q_LORA + KV_LORA + 2 * LANES
ROUTE_ROWS = 8
ROW_CHUNKS = D_MODEL // 2 // LANES
VMEM_LIMIT = 56 * 1024 * 1024

TOKEN_TILE = 512
ATTN_TILE = 512
MOE_TILE = 512

F32 = jnp.float32
BF16 = jnp.bfloat16
_NT = (((1,), (1,)), ((), ()))


def _params(*sem):
    return pltpu.CompilerParams(dimension_semantics=sem, vmem_limit_bytes=VMEM_LIMIT)


def _const_spec(shape):
    return pl.BlockSpec(shape, lambda *_: (0,) * len(shape))


def _dot(a, b):
    return jnp.dot(a, b, preferred_element_type=F32)


def _layer_norm(z, g, b):
    mu = jnp.mean(z, axis=-1, keepdims=True)
    zc = z - mu
    var = jnp.mean(zc * zc, axis=-1, keepdims=True)
    return zc * lax.rsqrt(var + LN_EPS) * g + b


def _rms_norm(z, g):
    return z * lax.rsqrt(jnp.mean(z * z, axis=-1, keepdims=True) + RMS_EPS) * g


def _mla_proj_kernel(x_ref, pos_ref, invf_ref, wd_ref, qn_ref, kvn_ref, wq_ref, wk_ref, wvt_ref,
                     q_out, k_out, vt_out):
    xb = x_ref[...].astype(BF16)
    lat = _dot(xb, wd_ref[...])
    cq = _rms_norm(lat[:, :Q_LORA], qn_ref[...]).astype(BF16)
    ckv = _rms_norm(lat[:, Q_LORA:Q_LORA + KV_LORA], kvn_ref[...]).astype(BF16)
    kr = lat[:, Q_LORA + KV_LORA:Q_LORA + KV_LORA + LANES]
    krr = lat[:, Q_LORA + KV_LORA + LANES:]
    ang = pos_ref[...].astype(F32) * invf_ref[...]
    cos = jnp.cos(ang)
    sin = jnp.sin(ang)
    k_rope = kr * cos + krr * sin
    ones_row = lax.broadcasted_iota(jnp.int32, (VT_ROWS, x_ref.shape[0]), 0) == V_HEAD
    for h in range(N_HEADS):
        qq = _dot(cq, wq_ref[:, h * 2 * LANES:(h + 1) * 2 * LANES])
        q_out[h] = ((qq[:, :LANES] * cos + qq[:, LANES:] * sin) * QK_SCALE).astype(BF16)
        k_out[h] = (_dot(ckv, wk_ref[:, h * LANES:(h + 1) * LANES]) + k_rope).astype(BF16)
        vt = lax.dot_general(wvt_ref[h * VT_ROWS:(h + 1) * VT_ROWS, :], ckv, _NT,
                             preferred_element_type=F32)
        vt_out[h, 0] = jnp.where(ones_row, 1.0, vt).astype(BF16)


def _mla_proj(x, pos_col, invf, wd, qn, kvn, wq, wk, wvt):
    s = x.shape[0]
    ts = min(ATTN_TILE, s)
    head_spec = pl.BlockSpec((N_HEADS, ts, LANES), lambda i: (0, i, 0))
    out_sds = jax.ShapeDtypeStruct((N_HEADS, s, LANES), BF16)
    return pl.pallas_call(
        _mla_proj_kernel,
        grid=(s // ts,),
        in_specs=[
            pl.BlockSpec((ts, D_MODEL), lambda i: (i, 0)),
            pl.BlockSpec((ts, 1), lambda i: (i, 0)),
            _const_spec((1, LANES)),
            _const_spec((D_MODEL, LAT_W)),
            _const_spec((1, Q_LORA)),
            _const_spec((1, KV_LORA)),
            _const_spec((Q_LORA, N_HEADS * 2 * LANES)),
            _const_spec((KV_LORA, N_HEADS * LANES)),
            _const_spec((N_HEADS * VT_ROWS, KV_LORA)),
        ],
        out_specs=[head_spec, head_spec,
                   pl.BlockSpec((N_HEADS, 1, VT_ROWS, ts), lambda i: (0, i, 0, 0))],
        out_shape=[out_sds, out_sds,
                   jax.ShapeDtypeStruct((N_HEADS, s // ts, VT_ROWS, ts), BF16)],
        compiler_params=_params("parallel"),
        name="mla_proj",
    )(x, pos_col, invf, wd, qn, kvn, wq, wk, wvt)


def _sublane_allmax(part):
    for shift in (4, 2, 1):
        part = jnp.maximum(part, pltpu.roll(part, shift, 0))
    return part


def _attn_kernel(q_ref, k_ref, vt_ref, o_ref, s_a, s_b, mx_a, mx_b, acc_sc, m_sc, *, t):
    qi = pl.program_id(1)
    q = q_ref[0]
    groups = t // SUBLANES
    m_sc[...] = jnp.full(m_sc.shape, -jnp.inf, F32)
    acc_sc[...] = jnp.zeros(acc_sc.shape, F32)

    def scores(j, s_dst, mx_dst, diagonal):
        k = k_ref[0, pl.ds(pl.multiple_of(j * t, t), t), :]
        s = lax.dot_general(k, q, _NT, preferred_element_type=F32)
        if diagonal:
            key = lax.broadcasted_iota(jnp.int32, s.shape, 0)
            qry = lax.broadcasted_iota(jnp.int32, s.shape, 1)
            s = jnp.where(key <= qry, s, -jnp.inf)
        s_dst[...] = s
        mx_dst[...] = jnp.max(s.reshape(groups, SUBLANES, t), axis=0)

    def softmax_pv(j, s_src, mx_src):
        m_prev = m_sc[...]
        m_new = jnp.maximum(m_prev, _sublane_allmax(mx_src[...]))
        alpha = jnp.exp2(m_prev - m_new)
        p = jnp.exp2(s_src[...].reshape(groups, SUBLANES, t) - m_new[None])
        pv = _dot(vt_ref[0, j], p.reshape(t, t).astype(BF16))
        acc = acc_sc[...].reshape(VT_ROWS // SUBLANES, SUBLANES, t) * alpha[None]
        acc_sc[...] = acc.reshape(VT_ROWS, t) + pv
        m_sc[...] = m_new

    scores(0, s_a, mx_a, False)

    @pl.when(qi == 0)
    def _():
        scores(0, s_a, mx_a, True)

    def pair(i, carry):
        j = 2 * i
        scores(j + 1, s_b, mx_b, False)
        softmax_pv(j, s_a, mx_a)
        scores(j + 2, s_a, mx_a, False)
        softmax_pv(j + 1, s_b, mx_b)
        return carry

    n_pairs = jnp.maximum((qi - 1) // 2, 0)
    lax.fori_loop(0, n_pairs, pair, 0)
    j0 = 2 * n_pairs

    @pl.when(jnp.logical_and(qi > 0, qi % 2 == 1))
    def _():
        scores(j0 + 1, s_b, mx_b, True)
        softmax_pv(j0, s_a, mx_a)
        softmax_pv(j0 + 1, s_b, mx_b)

    @pl.when(jnp.logical_and(qi > 0, qi % 2 == 0))
    def _():
        scores(j0 + 1, s_b, mx_b, False)
        softmax_pv(j0, s_a, mx_a)
        scores(j0 + 2, s_a, mx_a, True)
        softmax_pv(j0 + 1, s_b, mx_b)
        softmax_pv(j0 + 2, s_a, mx_a)

    @pl.when(qi == 0)
    def _():
        softmax_pv(0, s_a, mx_a)

    acc = acc_sc[...]
    o_t = acc[:V_HEAD] / acc[V_HEAD:V_HEAD + 1, :]
    o_t = jnp.concatenate([o_t, jnp.zeros((LANES - V_HEAD, t), F32)], axis=0)
    o_ref[...] = o_t.T.astype(o_ref.dtype)


def _attention(q, k, vt):
    s = q.shape[1]
    t = vt.shape[-1]
    return pl.pallas_call(
        functools.partial(_attn_kernel, t=t),
        grid=(N_HEADS, s // t),
        in_specs=[
            pl.BlockSpec((1, t, LANES), lambda h, i: (h, i, 0)),
            pl.BlockSpec((1, s, LANES), lambda h, i: (h, 0, 0)),
            pl.BlockSpec((1, s // t, VT_ROWS, t), lambda h, i: (h, 0, 0, 0)),
        ],
        out_specs=pl.BlockSpec((t, LANES), lambda h, i: (i, h)),
        out_shape=jax.ShapeDtypeStruct((s, N_HEADS * LANES), BF16),
        scratch_shapes=[
            pltpu.VMEM((t, t), F32),
            pltpu.VMEM((t, t), F32),
            pltpu.VMEM((SUBLANES, t), F32),
            pltpu.VMEM((SUBLANES, t), F32),
            pltpu.VMEM((VT_ROWS, t), F32),
            pltpu.VMEM((SUBLANES, t), F32),
        ],
        compiler_params=_params("parallel", "parallel"),
        name="mla_attention",
    )(q, k, vt)


def _top2_of_4(a0, a1, a2, a3):
    hi1, lo1 = jnp.maximum(a0, a1), jnp.minimum(a0, a1)
    hi2, lo2 = jnp.maximum(a2, a3), jnp.minimum(a2, a3)
    first = jnp.maximum(hi1, hi2)
    second = jnp.maximum(jnp.minimum(hi1, hi2), jnp.maximum(lo1, lo2))
    return first + second


def _argmax_first(vals):
    best, idx = vals[0], jnp.zeros_like(vals[0])
    for i in range(1, len(vals)):
        take = vals[i] > best
        best = jnp.where(take, vals[i], best)
        idx = jnp.where(take, float(i), idx)
    return idx, best


def _pick(idx, rows):
    out = rows[-1]
    for i in range(len(rows) - 2, -1, -1):
        out = jnp.where(idx == float(i), rows[i], out)
    return out


def _store_packed_rows(ref, v):
    n, d = v.shape
    w = d // 2
    lo = lax.bitcast_convert_type(v[:, :w].astype(BF16).astype(F32), jnp.uint32)
    hi = lax.bitcast_convert_type(v[:, w:].astype(BF16).astype(F32), jnp.uint32)
    packed = (lo >> 16) | hi
    for c in range(ROW_CHUNKS):
        ref[pl.ds(c, n, stride=ROW_CHUNKS), :] = packed[:, c * LANES:(c + 1) * LANES]


def _load_packed_rows(ref):
    n = ref.shape[0] // ROW_CHUNKS
    chunks = [ref[pl.ds(c, n, stride=ROW_CHUNKS), :] for c in range(ROW_CHUNKS)]
    lo = [lax.bitcast_convert_type(u << 16, F32) for u in chunks]
    hi = [lax.bitcast_convert_type(u & jnp.uint32(0xFFFF0000), F32) for u in chunks]
    return jnp.concatenate(lo + hi, axis=1)


def _route_rows(y, rw_ref, rbias_ref, tri_ref, run_sc):
    y_hi = y.astype(BF16)
    y_lo = (y - y_hi.astype(F32)).astype(BF16)
    prod = _dot(y_hi, rw_ref[...]) + _dot(y_lo, rw_ref[...])
    prod_t = prod.T
    logits = prod_t[:N_EXPERTS] + prod_t[N_EXPERTS:2 * N_EXPERTS]
    scores = 1.0 / (1.0 + jnp.exp(-logits))
    biased = scores + rbias_ref[...]
    b = [biased[e:e + 1] for e in range(N_EXPERTS)]
    sc = [scores[e:e + 1] for e in range(N_EXPERTS)]
    group_scores = [_top2_of_4(*b[g * 4:g * 4 + 4]) for g in range(N_GROUPS)]
    g_idx, _ = _argmax_first(group_scores)
    in_b = [_pick(g_idx, [b[g * 4 + j] for g in range(N_GROUPS)]) for j in range(EXPERTS_PER_GROUP)]
    in_s = [_pick(g_idx, [sc[g * 4 + j] for g in range(N_GROUPS)]) for j in range(EXPERTS_PER_GROUP)]
    l1, _ = _argmax_first(in_b)
    rest = [jnp.where(l1 == float(j), -jnp.inf, in_b[j]) for j in range(EXPERTS_PER_GROUP)]
    l2, _ = _argmax_first(rest)
    w1 = _pick(l1, in_s)
    w2 = _pick(l2, in_s)
    wsum = w1 + w2
    e1 = g_idx * float(EXPERTS_PER_GROUP) + l1
    e2 = g_idx * float(EXPERTS_PER_GROUP) + l2
    eid = lax.broadcasted_iota(jnp.int32, logits.shape, 0).astype(F32)
    hit1 = eid == e1
    hit2 = eid == e2
    member = jnp.where(jnp.logical_or(hit1, hit2), 1.0, 0.0)
    run = run_sc[...]
    before = _dot(member.astype(BF16), tri_ref[...]) + run[:, 0:1]
    r1 = jnp.sum(jnp.where(hit1, before, 0.0), axis=0, keepdims=True)
    r2 = jnp.sum(jnp.where(hit2, before, 0.0), axis=0, keepdims=True)
    run_sc[...] = run + jnp.sum(member, axis=1, keepdims=True)
    zero = jnp.zeros_like(e1)
    return jnp.concatenate([e1, e2, w1 / wsum, w2 / wsum, r1, r2, zero, zero], axis=0)


def _mixer_epilogue(x, h, g_ref, b_ref, rw_ref, rbias_ref, tri_ref, y_out, yb_out, rt_out, cnt_out,
                    run_sc):
    @pl.when(pl.program_id(0) == 0)
    def _():
        run_sc[...] = jnp.zeros(run_sc.shape, F32)

    y = _layer_norm(DEEPNORM_ALPHA * x + h, g_ref[...], b_ref[...])
    y_out[...] = y
    _store_packed_rows(yb_out, y)
    rt_out[...] = _route_rows(y, rw_ref, rbias_ref, tri_ref, run_sc)
    cnt_out[...] = run_sc[...]


def _epilogue_specs(ts):
    in_specs = [
        _const_spec((1, D_MODEL)),
        _const_spec((1, D_MODEL)),
        _const_spec((D_MODEL, LANES)),
        _const_spec((N_EXPERTS, 1)),
        _const_spec((ts, ts)),
    ]
    out_specs = [
        pl.BlockSpec((ts, D_MODEL), lambda i: (i, 0)),
        pl.BlockSpec((ROW_CHUNKS * ts, LANES), lambda i: (i, 0)),
        pl.BlockSpec((ROUTE_ROWS, ts), lambda i: (0, i)),
        _const_spec((N_EXPERTS, LANES)),
    ]
    return in_specs, out_specs


def _epilogue_shapes(s):
    return [
        jax.ShapeDtypeStruct((s, D_MODEL), F32),
        jax.ShapeDtypeStruct((ROW_CHUNKS * s, LANES), jnp.uint32),
        jax.ShapeDtypeStruct((ROUTE_ROWS, s), F32),
        jax.ShapeDtypeStruct((N_EXPERTS, LANES), F32),
    ]


_RUN_SCRATCH = pltpu.VMEM((N_EXPERTS, LANES), F32)


def _attn_out_kernel(x_ref, o_ref, wo_ref, *epilogue_refs):
    h = _dot(o_ref[...], wo_ref[...])
    _mixer_epilogue(x_ref[...], h, *epilogue_refs)


def _attn_out(x, o, wo, g, b, rw, rbias, tri):
    s = x.shape[0]
    ts = min(TOKEN_TILE, s)
    ep_in, ep_out = _epilogue_specs(ts)
    return pl.pallas_call(
        _attn_out_kernel,
        grid=(s // ts,),
        in_specs=[
            pl.BlockSpec((ts, D_MODEL), lambda i: (i, 0)),
            pl.BlockSpec((ts, N_HEADS * LANES), lambda i: (i, 0)),
            _const_spec((N_HEADS * LANES, D_MODEL)),
        ] + ep_in,
        out_specs=ep_out,
        out_shape=_epilogue_shapes(s),
        scratch_shapes=[_RUN_SCRATCH],
        compiler_params=_params("arbitrary"),
        name="attn_out_ln_route",
    )(x, o, wo, g, b, rw, rbias, tri)


def _conv_kernel(x_ref, win_ref, ck_ref, wout_ref, g_ref, b_ref, rw_ref, rbias_ref, tri_ref,
                 y_out, yb_out, rt_out, cnt_out, u_sc, run_sc, *, ts):
    @pl.when(pl.program_id(0) == 0)
    def _():
        u_sc[0:8, :] = jnp.zeros((8, D_MODEL), F32)

    x = x_ref[...]
    xb = x.astype(BF16)
    c_gate = _dot(xb, win_ref[:, D_MODEL:2 * D_MODEL])
    u = c_gate * _dot(xb, win_ref[:, 2 * D_MODEL:])
    u_sc[8:8 + ts, :] = u
    ck = ck_ref[...]
    conv = u_sc[6:6 + ts, :] * ck[0:1] + u_sc[7:7 + ts, :] * ck[1:2] + u * ck[2:3]
    u_sc[0:8, :] = u_sc[ts:ts + 8, :]
    b_gate = _dot(xb, win_ref[:, :D_MODEL])
    h = _dot((b_gate * conv).astype(BF16), wout_ref[...])
    _mixer_epilogue(x, h, g_ref, b_ref, rw_ref, rbias_ref, tri_ref, y_out, yb_out, rt_out, cnt_out,
                    run_sc)


def _conv_mixer(x, win, ck, wout, g, b, rw, rbias, tri):
    s = x.shape[0]
    ts = min(TOKEN_TILE, s)
    ep_in, ep_out = _epilogue_specs(ts)
    return pl.pallas_call(
        functools.partial(_conv_kernel, ts=ts),
        grid=(s // ts,),
        in_specs=[
            pl.BlockSpec((ts, D_MODEL), lambda i: (i, 0)),
            _const_spec((D_MODEL, 3 * D_MODEL)),
            _const_spec((CONV_W, D_MODEL)),
            _const_spec((D_MODEL, D_MODEL)),
        ] + ep_in,
        out_specs=ep_out,
        out_shape=_epilogue_shapes(s),
        scratch_shapes=[pltpu.VMEM((ts + 8, D_MODEL), F32), _RUN_SCRATCH],
        compiler_params=_params("arbitrary"),
        name="conv_mixer_ln_route",
    )(x, win, ck, wout, g, b, rw, rbias, tri)


def _tile_plan(counts, tm, n_tiles_max):
    counts = counts[:, 0].astype(jnp.int32)
    tiles_per_expert = (counts + tm - 1) // tm
    tile_end = jnp.cumsum(tiles_per_expert)
    row_start = (tile_end - tiles_per_expert) * tm
    n_tiles = tile_end[-1:]
    tile = jnp.minimum(jnp.arange(n_tiles_max, dtype=jnp.int32), n_tiles - 1)
    tile_expert = jnp.searchsorted(tile_end, tile, side="right").astype(jnp.int32)
    return row_start.astype(jnp.int32), tile_expert, n_tiles.astype(jnp.int32)


def _row(ref, r):
    return ref.at[pl.ds(pl.multiple_of(r * ROW_CHUNKS, ROW_CHUNKS), ROW_CHUNKS)]


def _dispatch_kernel(pos_ref, yb_ref, zeros_ref, xs_ref, sem, *, ts):
    del zeros_ref

    def issue(t, carry):
        for k in range(2):
            pltpu.make_async_copy(_row(yb_ref, t), _row(xs_ref, pos_ref[0, k, t]), sem).start(priority=k)
        return carry

    lax.fori_loop(0, ts, issue, 0, unroll=8)
    for _ in range(2):
        pltpu.make_async_copy(yb_ref, xs_ref.at[pl.ds(0, ROW_CHUNKS * ts)], sem).wait()


def _pos_spec(ts):
    return pl.BlockSpec((1, 2, ts), lambda i: (i, 0, 0), memory_space=pltpu.SMEM)


def _dispatch(pos, yb, n_rows):
    ts = pos.shape[-1]
    s = yb.shape[0] // ROW_CHUNKS
    zeros = jnp.zeros((ROW_CHUNKS * n_rows, LANES), jnp.uint32)
    return pl.pallas_call(
        functools.partial(_dispatch_kernel, ts=ts),
        grid=(s // ts,),
        in_specs=[
            _pos_spec(ts),
            pl.BlockSpec((ROW_CHUNKS * ts, LANES), lambda i: (i, 0)),
            pl.BlockSpec(memory_space=pl.ANY),
        ],
        out_specs=pl.BlockSpec(memory_space=pl.ANY),
        out_shape=jax.ShapeDtypeStruct(zeros.shape, jnp.uint32),
        scratch_shapes=[pltpu.SemaphoreType.DMA],
        input_output_aliases={2: 0},
        compiler_params=_params("arbitrary"),
        name="moe_dispatch",
    )(pos, yb, zeros)


def _expert_kernel(te_ref, nt_ref, xs_ref, wg_ref, wu_ref, wd_ref, ys_ref, wg_b, wu_b, wd_b):
    i = pl.program_id(0)

    @pl.when(i < nt_ref[0])
    def _():
        @pl.when(jnp.logical_or(i == 0, te_ref[i] != te_ref[jnp.maximum(i - 1, 0)]))
        def _():
            wg_b[...] = wg_ref[0, 0].astype(BF16)
            wu_b[...] = wu_ref[0, 0].astype(BF16)
            wd_b[...] = wd_ref[0, 0].astype(BF16)

        x = _load_packed_rows(xs_ref).astype(BF16)
        hg = _dot(x, wg_b[...])
        hu = _dot(x, wu_b[...])
        hidden = (hg * (1.0 / (1.0 + jnp.exp(-hg)))) * hu
        _store_packed_rows(ys_ref, _dot(hidden.astype(BF16), wd_b[...]))

    @pl.when(i >= nt_ref[0])
    def _():
        ys_ref[...] = jnp.zeros(ys_ref.shape, ys_ref.dtype)


def _experts(tile_expert, n_tiles, xs, w_gate, w_up, w_down, layer, tm):
    n_rows = xs.shape[0] // ROW_CHUNKS
    block = (ROW_CHUNKS * tm, LANES)
    weight = lambda i, te, nt: (layer, te[i], 0, 0)
    return pl.pallas_call(
        _expert_kernel,
        grid_spec=pltpu.PrefetchScalarGridSpec(
            num_scalar_prefetch=2,
            grid=(n_rows // tm,),
            in_specs=[
                pl.BlockSpec(block, lambda i, te, nt: (jnp.minimum(i, nt[0] - 1), 0)),
                pl.BlockSpec((1, 1, D_MODEL, D_EXPERT), weight),
                pl.BlockSpec((1, 1, D_MODEL, D_EXPERT), weight),
                pl.BlockSpec((1, 1, D_EXPERT, D_MODEL), weight),
            ],
            out_specs=pl.BlockSpec(block, lambda i, te, nt: (i, 0)),
            scratch_shapes=[
                pltpu.VMEM((D_MODEL, D_EXPERT), BF16),
                pltpu.VMEM((D_MODEL, D_EXPERT), BF16),
                pltpu.VMEM((D_EXPERT, D_MODEL), BF16),
            ],
        ),
        out_shape=jax.ShapeDtypeStruct(xs.shape, jnp.uint32),
        compiler_params=_params("arbitrary"),
        name="moe_experts",
    )(tile_expert, n_tiles, xs, w_gate, w_up, w_down)


def _combine_kernel(pos_ref, y_ref, rt_ref, ys_ref, g_ref, b_ref, out_ref, got1, got2, sem, *, ts):
    def issue(t, carry):
        for k, got in enumerate((got1, got2)):
            pltpu.make_async_copy(_row(ys_ref, pos_ref[0, k, t]), _row(got, t), sem).start(priority=k)
        return carry

    lax.fori_loop(0, ts, issue, 0, unroll=8)
    for got in (got1, got2):
        pltpu.make_async_copy(ys_ref.at[pl.ds(0, ROW_CHUNKS * ts)], got, sem).wait()
    rt = rt_ref[...]
    moe = rt[:, 2:3] * _load_packed_rows(got1) + rt[:, 3:4] * _load_packed_rows(got2)
    out_ref[...] = _layer_norm(DEEPNORM_ALPHA * y_ref[...] + moe, g_ref[...], b_ref[...])


def _combine(pos, y, rt_cols, ys, g, b):
    s = y.shape[0]
    ts = pos.shape[-1]
    tok = lambda i: (i, 0)
    return pl.pallas_call(
        functools.partial(_combine_kernel, ts=ts),
        grid=(s // ts,),
        in_specs=[
            _pos_spec(ts),
            pl.BlockSpec((ts, D_MODEL), tok),
            pl.BlockSpec((ts, ROUTE_ROWS), tok),
            pl.BlockSpec(memory_space=pl.ANY),
            _const_spec((1, D_MODEL)),
            _const_spec((1, D_MODEL)),
        ],
        out_specs=pl.BlockSpec((ts, D_MODEL), tok),
        out_shape=jax.ShapeDtypeStruct((s, D_MODEL), F32),
        scratch_shapes=[
            pltpu.VMEM((ROW_CHUNKS * ts, LANES), jnp.uint32),
            pltpu.VMEM((ROW_CHUNKS * ts, LANES), jnp.uint32),
            pltpu.SemaphoreType.DMA,
        ],
        compiler_params=_params("arbitrary"),
        name="moe_combine_ln",
    )(pos, y, rt_cols, ys, g, b)


def _moe(y, yb, rt, counts, w_gate, w_up, w_down, layer, g, b):
    s = y.shape[0]
    ts = min(TOKEN_TILE, s)
    tm = MOE_TILE
    n_rows = 2 * s + N_EXPERTS * tm
    row_start, tile_expert, n_tiles = _tile_plan(counts, tm, n_rows // tm)
    pos = row_start[rt[0:2].astype(jnp.int32)] + rt[4:6].astype(jnp.int32)
    pos = pos.reshape(2, s // ts, ts).transpose(1, 0, 2)
    xs = _dispatch(pos, yb, n_rows)
    ys = _experts(tile_expert, n_tiles, xs, w_gate, w_up, w_down, layer, tm)
    return _combine(pos, y, rt.T, ys, g, b)


def _rope_lane_pad(t):
    pad = [(0, 0)] * (t.ndim - 1)
    return jnp.pad(t, pad + [(QK_NOPE, LANES - QK_NOPE - QK_ROPE)])


def _rotate_half(t):
    return jnp.concatenate([-t[..., HALF_ROPE:], t[..., :HALF_ROPE]], axis=-1)


def _prep_mla(w_dqkv, w_uq, w_ukv, w_o):
    w_kr = w_dqkv[:, Q_LORA + KV_LORA:]
    wd = jnp.concatenate(
        [w_dqkv[:, :Q_LORA + KV_LORA], _rope_lane_pad(w_kr), _rope_lane_pad(_rotate_half(w_kr))],
        axis=1).astype(BF16)
    uq = w_uq.reshape(Q_LORA, N_HEADS, QK_NOPE + QK_ROPE)
    zq = jnp.zeros((Q_LORA, N_HEADS, LANES - QK_NOPE - QK_ROPE), F32)
    q_plain = jnp.concatenate([uq, zq], axis=-1)
    q_rot = jnp.concatenate(
        [jnp.zeros((Q_LORA, N_HEADS, QK_NOPE), F32), _rotate_half(uq[..., QK_NOPE:]), zq], axis=-1)
    wq = jnp.concatenate([q_plain, q_rot], axis=-1).reshape(Q_LORA, N_HEADS * 2 * LANES).astype(BF16)
    ukv = w_ukv.reshape(KV_LORA, N_HEADS, QK_NOPE + V_HEAD)
    wk = jnp.pad(ukv[..., :QK_NOPE], ((0, 0), (0, 0), (0, LANES - QK_NOPE)))
    wk = wk.reshape(KV_LORA, N_HEADS * LANES).astype(BF16)
    wvt = jnp.pad(ukv[..., QK_NOPE:], ((0, 0), (0, 0), (0, VT_ROWS - V_HEAD)))
    wvt = wvt.reshape(KV_LORA, N_HEADS * VT_ROWS).T.astype(BF16)
    wo = jnp.pad(w_o.reshape(N_HEADS, V_HEAD, D_MODEL), ((0, 0), (0, LANES - V_HEAD), (0, 0)))
    wo = wo.reshape(N_HEADS * LANES, D_MODEL).astype(BF16)
    return wd, wq, wk, wvt, wo


def _prep_router(router_w, router_bias):
    hi = router_w.astype(BF16)
    lo = (router_w - hi.astype(F32)).astype(BF16)
    rw = jnp.concatenate([hi, lo, jnp.zeros((D_MODEL, LANES - 2 * N_EXPERTS), BF16)], axis=1)
    return rw, router_bias.reshape(N_EXPERTS, 1).astype(F32)


def kernel(x, positions, mla_w_dqkv, mla_q_norm, mla_kv_norm, mla_w_uq, mla_w_ukv, mla_w_o,
           conv_w_in, conv_kernel, conv_w_out, router_w, router_bias,
           moe_w_gate, moe_w_up, moe_w_down, ln_mix_g, ln_mix_b, ln_ffn_g, ln_ffn_b):
    batch, s, _ = x.shape
    rw, rbias = _prep_router(router_w, router_bias)
    inv_freq = ROPE_THETA ** (-jnp.arange(0, QK_ROPE, 2, dtype=F32) / QK_ROPE)
    invf = _rope_lane_pad(jnp.concatenate([inv_freq, inv_freq]))[None, :]
    ts = min(TOKEN_TILE, s)
    tri = jnp.triu(jnp.ones((ts, ts), BF16), k=1)
    row = lambda t: t.reshape(1, -1).astype(F32)

    outs = []
    for bi in range(batch):
        xc = x[bi]
        pos_col = positions[bi].reshape(s, 1)
        for i in range(DEPTH):
            j = i // 2
            if i % 2 == 0:
                wd, wq, wk, wvt, wo = _prep_mla(mla_w_dqkv[j], mla_w_uq[j], mla_w_ukv[j], mla_w_o[j])
                q, k, vt = _mla_proj(xc, pos_col, invf, wd, row(mla_q_norm[j]), row(mla_kv_norm[j]),
                                     wq, wk, wvt)
                o = _attention(q, k, vt)
                y, yb, rt, cnt = _attn_out(xc, o, wo, row(ln_mix_g[i]), row(ln_mix_b[i]), rw, rbias, tri)
            else:
                y, yb, rt, cnt = _conv_mixer(xc, conv_w_in[j].astype(BF16), conv_kernel[j].astype(F32),
                                             conv_w_out[j].astype(BF16), row(ln_mix_g[i]),
                                             row(ln_mix_b[i]), rw, rbias, tri)
            xc = _moe(y, yb, rt, cnt, moe_w_gate, moe_w_up, moe_w_down, i,
                      row(ln_ffn_g[i]), row(ln_ffn_b[i]))
        outs.append(xc)
    return jnp.stack(outs, axis=0)
```

```python
import functools

import jax
import jax.numpy as jnp
from jax import lax
from jax.experimental import pallas as pl
from jax.experimental.pallas import tpu as pltpu

D_MODEL = 1024
DEPTH = 4
N_HEADS = 16
QK_NOPE = 64
QK_ROPE = 32
V_HEAD = 64
Q_LORA = 256
KV_LORA = 128
ROPE_THETA = 10000.0
CONV_W = 3
N_EXPERTS = 16
N_GROUPS = 4
EXPERTS_PER_GROUP = N_EXPERTS // N_GROUPS
D_EXPERT = 512
DEEPNORM_ALPHA = float((2 * DEPTH) ** 0.25)
LN_EPS = 1e-5
RMS_EPS = 1e-6
QK_SCALE = (QK_NOPE + QK_ROPE) ** -0.5 * 1.4426950408889634

LANES = 128
SUBLANES = 8
VT_ROWS = 80
HALF_ROPE = QK_ROPE // 2
LAT_W = Q_LORA + KV_LORA + 2 * LANES
ROUTE_ROWS = 8
ROW_CHUNKS = D_MODEL // LANES
VMEM_LIMIT = 56 * 1024 * 1024

TOKEN_TILE = 512
ATTN_TILE = 512
MOE_TILE = 512

F32 = jnp.float32
BF16 = jnp.bfloat16
_NT = (((1,), (1,)), ((), ()))


def _params(*sem):
    return pltpu.CompilerParams(dimension_semantics=sem, vmem_limit_bytes=VMEM_LIMIT)


def _const_spec(shape):
    return pl.BlockSpec(shape, lambda *_: (0,) * len(shape))


def _dot(a, b):
    return jnp.dot(a, b, preferred_element_type=F32)


def _layer_norm(z, g, b):
    mu = jnp.mean(z, axis=-1, keepdims=True)
    zc = z - mu
    var = jnp.mean(zc * zc, axis=-1, keepdims=True)
    return zc * lax.rsqrt(var + LN_EPS) * g + b


def _rms_norm(z, g):
    return z * lax.rsqrt(jnp.mean(z * z, axis=-1, keepdims=True) + RMS_EPS) * g


def _mla_proj_kernel(x_ref, pos_ref, invf_ref, wd_ref, qn_ref, kvn_ref, wq_ref, wk_ref, wvt_ref,
                     q_out, k_out, vt_out):
    xb = x_ref[...].astype(BF16)
    lat = _dot(xb, wd_ref[...])
    cq = _rms_norm(lat[:, :Q_LORA], qn_ref[...]).astype(BF16)
    ckv = _rms_norm(lat[:, Q_LORA:Q_LORA + KV_LORA], kvn_ref[...]).astype(BF16)
    kr = lat[:, Q_LORA + KV_LORA:Q_LORA + KV_LORA + LANES]
    krr = lat[:, Q_LORA + KV_LORA + LANES:]
    ang = pos_ref[...].astype(F32) * invf_ref[...]
    cos = jnp.cos(ang)
    sin = jnp.sin(ang)
    k_rope = kr * cos + krr * sin
    ones_row = lax.broadcasted_iota(jnp.int32, (VT_ROWS, x_ref.shape[0]), 0) == V_HEAD
    for h in range(N_HEADS):
        qq = _dot(cq, wq_ref[:, h * 2 * LANES:(h + 1) * 2 * LANES])
        q_out[h] = ((qq[:, :LANES] * cos + qq[:, LANES:] * sin) * QK_SCALE).astype(BF16)
        k_out[h] = (_dot(ckv, wk_ref[:, h * LANES:(h + 1) * LANES]) + k_rope).astype(BF16)
        vt = lax.dot_general(wvt_ref[h * VT_ROWS:(h + 1) * VT_ROWS, :], ckv, _NT,
                             preferred_element_type=F32)
        vt_out[h, 0] = jnp.where(ones_row, 1.0, vt).astype(BF16)


def _mla_proj(x, pos_col, invf, wd, qn, kvn, wq, wk, wvt):
    s = x.shape[0]
    ts = min(ATTN_TILE, s)
    head_spec = pl.BlockSpec((N_HEADS, ts, LANES), lambda i: (0, i, 0))
    out_sds = jax.ShapeDtypeStruct((N_HEADS, s, LANES), BF16)
    return pl.pallas_call(
        _mla_proj_kernel,
        grid=(s // ts,),
        in_specs=[
            pl.BlockSpec((ts, D_MODEL), lambda i: (i, 0)),
            pl.BlockSpec((ts, 1), lambda i: (i, 0)),
            _const_spec((1, LANES)),
            _const_spec((D_MODEL, LAT_W)),
            _const_spec((1, Q_LORA)),
            _const_spec((1, KV_LORA)),
            _const_spec((Q_LORA, N_HEADS * 2 * LANES)),
            _const_spec((KV_LORA, N_HEADS * LANES)),
            _const_spec((N_HEADS * VT_ROWS, KV_LORA)),
        ],
        out_specs=[head_spec, head_spec,
                   pl.BlockSpec((N_HEADS, 1, VT_ROWS, ts), lambda i: (0, i, 0, 0))],
        out_shape=[out_sds, out_sds,
                   jax.ShapeDtypeStruct((N_HEADS, s // ts, VT_ROWS, ts), BF16)],
        compiler_params=_params("parallel"),
        name="mla_proj",
    )(x, pos_col, invf, wd, qn, kvn, wq, wk, wvt)


def _sublane_allmax(part):
    for shift in (4, 2, 1):
        part = jnp.maximum(part, pltpu.roll(part, shift, 0))
    return part


def _attn_kernel(q_ref, k_ref, vt_ref, o_ref, s_a, s_b, mx_a, mx_b, acc_sc, m_sc, *, t):
    qi = pl.program_id(1)
    q = q_ref[0]
    groups = t // SUBLANES
    m_sc[...] = jnp.full(m_sc.shape, -jnp.inf, F32)
    acc_sc[...] = jnp.zeros(acc_sc.shape, F32)

    def scores(j, s_dst, mx_dst, diagonal):
        k = k_ref[0, pl.ds(pl.multiple_of(j * t, t), t), :]
        s = lax.dot_general(k, q, _NT, preferred_element_type=F32)
        if diagonal:
            key = lax.broadcasted_iota(jnp.int32, s.shape, 0)
            qry = lax.broadcasted_iota(jnp.int32, s.shape, 1)
            s = jnp.where(key <= qry, s, -jnp.inf)
        s_dst[...] = s
        mx_dst[...] = jnp.max(s.reshape(groups, SUBLANES, t), axis=0)

    def softmax_pv(j, s_src, mx_src):
        m_prev = m_sc[...]
        m_new = jnp.maximum(m_prev, _sublane_allmax(mx_src[...]))
        alpha = jnp.exp2(m_prev - m_new)
        p = jnp.exp2(s_src[...].reshape(groups, SUBLANES, t) - m_new[None])
        pv = _dot(vt_ref[0, j], p.reshape(t, t).astype(BF16))
        acc = acc_sc[...].reshape(VT_ROWS // SUBLANES, SUBLANES, t) * alpha[None]
        acc_sc[...] = acc.reshape(VT_ROWS, t) + pv
        m_sc[...] = m_new

    scores(0, s_a, mx_a, False)

    @pl.when(qi == 0)
    def _():
        scores(0, s_a, mx_a, True)

    def pair(i, carry):
        j = 2 * i
        scores(j + 1, s_b, mx_b, False)
        softmax_pv(j, s_a, mx_a)
        scores(j + 2, s_a, mx_a, False)
        softmax_pv(j + 1, s_b, mx_b)
        return carry

    def quad(i, carry):
        pair(2 * i, carry)
        return pair(2 * i + 1, carry)

    n_pairs = jnp.maximum((qi - 1) // 2, 0)
    n_quads = n_pairs // 2
    lax.fori_loop(0, n_quads, quad, 0)
    lax.fori_loop(2 * n_quads, n_pairs, pair, 0)
    j0 = 2 * n_pairs

    @pl.when(jnp.logical_and(qi > 0, qi % 2 == 1))
    def _():
        scores(j0 + 1, s_b, mx_b, True)
        softmax_pv(j0, s_a, mx_a)
        softmax_pv(j0 + 1, s_b, mx_b)

    @pl.when(jnp.logical_and(qi > 0, qi % 2 == 0))
    def _():
        scores(j0 + 1, s_b, mx_b, False)
        softmax_pv(j0, s_a, mx_a)
        scores(j0 + 2, s_a, mx_a, True)
        softmax_pv(j0 + 1, s_b, mx_b)
        softmax_pv(j0 + 2, s_a, mx_a)

    @pl.when(qi == 0)
    def _():
        softmax_pv(0, s_a, mx_a)

    acc = acc_sc[...]
    o_t = acc[:V_HEAD] / acc[V_HEAD:V_HEAD + 1, :]
    o_t = jnp.concatenate([o_t, jnp.zeros((LANES - V_HEAD, t), F32)], axis=0)
    o_ref[...] = o_t.T.astype(o_ref.dtype)


def _attention(q, k, vt):
    s = q.shape[1]
    t = vt.shape[-1]
    return pl.pallas_call(
        functools.partial(_attn_kernel, t=t),
        grid=(N_HEADS, s // t),
        in_specs=[
            pl.BlockSpec((1, t, LANES), lambda h, i: (h, i, 0)),
            pl.BlockSpec((1, s, LANES), lambda h, i: (h, 0, 0)),
            pl.BlockSpec((1, s // t, VT_ROWS, t), lambda h, i: (h, 0, 0, 0)),
        ],
        out_specs=pl.BlockSpec((t, LANES), lambda h, i: (i, h)),
        out_shape=jax.ShapeDtypeStruct((s, N_HEADS * LANES), BF16),
        scratch_shapes=[
            pltpu.VMEM((t, t), F32),
            pltpu.VMEM((t, t), F32),
            pltpu.VMEM((SUBLANES, t), F32),
            pltpu.VMEM((SUBLANES, t), F32),
            pltpu.VMEM((VT_ROWS, t), F32),
            pltpu.VMEM((SUBLANES, t), F32),
        ],
        compiler_params=_params("parallel", "parallel"),
        name="mla_attention",
    )(q, k, vt)


def _top2_of_4(a0, a1, a2, a3):
    hi1, lo1 = jnp.maximum(a0, a1), jnp.minimum(a0, a1)
    hi2, lo2 = jnp.maximum(a2, a3), jnp.minimum(a2, a3)
    first = jnp.maximum(hi1, hi2)
    second = jnp.maximum(jnp.minimum(hi1, hi2), jnp.maximum(lo1, lo2))
    return first + second


def _argmax_first(vals):
    best, idx = vals[0], jnp.zeros_like(vals[0])
    for i in range(1, len(vals)):
        take = vals[i] > best
        best = jnp.where(take, vals[i], best)
        idx = jnp.where(take, float(i), idx)
    return idx, best


def _pick(idx, rows):
    out = rows[-1]
    for i in range(len(rows) - 2, -1, -1):
        out = jnp.where(idx == float(i), rows[i], out)
    return out


def _store_rows(ref, v):
    n = v.shape[0]
    for c in range(ROW_CHUNKS):
        ref[pl.ds(c, n, stride=ROW_CHUNKS), :] = v[:, c * LANES:(c + 1) * LANES]


def _load_rows(ref):
    n = ref.shape[0] // ROW_CHUNKS
    return jnp.concatenate(
        [ref[pl.ds(c, n, stride=ROW_CHUNKS), :] for c in range(ROW_CHUNKS)], axis=1)


def _route_rows(y, rw_ref, rbias_ref, tri_ref, run_sc):
    y_hi = y.astype(BF16)
    y_lo = (y - y_hi.astype(F32)).astype(BF16)
    prod = _dot(y_hi, rw_ref[...]) + _dot(y_lo, rw_ref[...])
    prod_t = prod.T
    logits = prod_t[:N_EXPERTS] + prod_t[N_EXPERTS:2 * N_EXPERTS]
    scores = 1.0 / (1.0 + jnp.exp(-logits))
    biased = scores + rbias_ref[...]
    b = [biased[e:e + 1] for e in range(N_EXPERTS)]
    sc = [scores[e:e + 1] for e in range(N_EXPERTS)]
    group_scores = [_top2_of_4(*b[g * 4:g * 4 + 4]) for g in range(N_GROUPS)]
    g_idx, _ = _argmax_first(group_scores)
    in_b = [_pick(g_idx, [b[g * 4 + j] for g in range(N_GROUPS)]) for j in range(EXPERTS_PER_GROUP)]
    in_s = [_pick(g_idx, [sc[g * 4 + j] for g in range(N_GROUPS)]) for j in range(EXPERTS_PER_GROUP)]
    l1, _ = _argmax_first(in_b)
    rest = [jnp.where(l1 == float(j), -jnp.inf, in_b[j]) for j in range(EXPERTS_PER_GROUP)]
    l2, _ = _argmax_first(rest)
    w1 = _pick(l1, in_s)
    w2 = _pick(l2, in_s)
    wsum = w1 + w2
    e1 = g_idx * float(EXPERTS_PER_GROUP) + l1
    e2 = g_idx * float(EXPERTS_PER_GROUP) + l2
    eid = lax.broadcasted_iota(jnp.int32, logits.shape, 0).astype(F32)
    hit1 = eid == e1
    hit2 = eid == e2
    member = jnp.where(jnp.logical_or(hit1, hit2), 1.0, 0.0)
    run = run_sc[...]
    before = _dot(member.astype(BF16), tri_ref[...]) + run[:, 0:1]
    r1 = jnp.sum(jnp.where(hit1, before, 0.0), axis=0, keepdims=True)
    r2 = jnp.sum(jnp.where(hit2, before, 0.0), axis=0, keepdims=True)
    run_sc[...] = run + jnp.sum(member, axis=1, keepdims=True)
    zero = jnp.zeros_like(e1)
    return jnp.concatenate([e1, e2, w1 / wsum, w2 / wsum, r1, r2, zero, zero], axis=0)


def _mixer_epilogue(x, h, g_ref, b_ref, rw_ref, rbias_ref, tri_ref, y_out, rt_out, cnt_out, run_sc):
    @pl.when(pl.program_id(0) == 0)
    def _():
        run_sc[...] = jnp.zeros(run_sc.shape, F32)

    y = _layer_norm(DEEPNORM_ALPHA * x + h, g_ref[...], b_ref[...])
    _store_rows(y_out, y)
    rt_out[...] = _route_rows(y, rw_ref, rbias_ref, tri_ref, run_sc)
    cnt_out[...] = run_sc[...]


def _epilogue_specs(ts):
    in_specs = [
        _const_spec((1, D_MODEL)),
        _const_spec((1, D_MODEL)),
        _const_spec((D_MODEL, LANES)),
        _const_spec((N_EXPERTS, 1)),
        _const_spec((ts, ts)),
    ]
    out_specs = [
        pl.BlockSpec((ROW_CHUNKS * ts, LANES), lambda i: (i, 0)),
        pl.BlockSpec((ROUTE_ROWS, ts), lambda i: (0, i)),
        _const_spec((N_EXPERTS, LANES)),
    ]
    return in_specs, out_specs


def _epilogue_shapes(s):
    return [
        jax.ShapeDtypeStruct((ROW_CHUNKS * s, LANES), F32),
        jax.ShapeDtypeStruct((ROUTE_ROWS, s), F32),
        jax.ShapeDtypeStruct((N_EXPERTS, LANES), F32),
    ]


_RUN_SCRATCH = pltpu.VMEM((N_EXPERTS, LANES), F32)


def _attn_out_kernel(x_ref, o_ref, wo_ref, *epilogue_refs):
    h = _dot(o_ref[...], wo_ref[...])
    _mixer_epilogue(x_ref[...], h, *epilogue_refs)


def _attn_out(x, o, wo, g, b, rw, rbias, tri):
    s = x.shape[0]
    ts = min(TOKEN_TILE, s)
    ep_in, ep_out = _epilogue_specs(ts)
    return pl.pallas_call(
        _attn_out_kernel,
        grid=(s // ts,),
        in_specs=[
            pl.BlockSpec((ts, D_MODEL), lambda i: (i, 0)),
            pl.BlockSpec((ts, N_HEADS * LANES), lambda i: (i, 0)),
            _const_spec((N_HEADS * LANES, D_MODEL)),
        ] + ep_in,
        out_specs=ep_out,
        out_shape=_epilogue_shapes(s),
        scratch_shapes=[_RUN_SCRATCH],
        compiler_params=_params("arbitrary"),
        name="attn_out_ln_route",
    )(x, o, wo, g, b, rw, rbias, tri)


def _conv_kernel(x_ref, win_ref, ck_ref, wout_ref, g_ref, b_ref, rw_ref, rbias_ref, tri_ref,
                 y_out, rt_out, cnt_out, u_sc, run_sc, *, ts):
    @pl.when(pl.program_id(0) == 0)
    def _():
        u_sc[0:8, :] = jnp.zeros((8, D_MODEL), F32)

    x = x_ref[...]
    xb = x.astype(BF16)
    c_gate = _dot(xb, win_ref[:, D_MODEL:2 * D_MODEL])
    u = c_gate * _dot(xb, win_ref[:, 2 * D_MODEL:])
    u_sc[8:8 + ts, :] = u
    ck = ck_ref[...]
    conv = u_sc[6:6 + ts, :] * ck[0:1] + u_sc[7:7 + ts, :] * ck[1:2] + u * ck[2:3]
    u_sc[0:8, :] = u_sc[ts:ts + 8, :]
    b_gate = _dot(xb, win_ref[:, :D_MODEL])
    h = _dot((b_gate * conv).astype(BF16), wout_ref[...])
    _mixer_epilogue(x, h, g_ref, b_ref, rw_ref, rbias_ref, tri_ref, y_out, rt_out, cnt_out, run_sc)


def _conv_mixer(x, win, ck, wout, g, b, rw, rbias, tri):
    s = x.shape[0]
    ts = min(TOKEN_TILE, s)
    ep_in, ep_out = _epilogue_specs(ts)
    return pl.pallas_call(
        functools.partial(_conv_kernel, ts=ts),
        grid=(s // ts,),
        in_specs=[
            pl.BlockSpec((ts, D_MODEL), lambda i: (i, 0)),
            _const_spec((D_MODEL, 3 * D_MODEL)),
            _const_spec((CONV_W, D_MODEL)),
            _const_spec((D_MODEL, D_MODEL)),
        ] + ep_in,
        out_specs=ep_out,
        out_shape=_epilogue_shapes(s),
        scratch_shapes=[pltpu.VMEM((ts + 8, D_MODEL), F32), _RUN_SCRATCH],
        compiler_params=_params("arbitrary"),
        name="conv_mixer_ln_route",
    )(x, win, ck, wout, g, b, rw, rbias, tri)


def _tile_plan(counts, tm, n_tiles_max):
    counts = counts[:, 0].astype(jnp.int32)
    tiles_per_expert = (counts + tm - 1) // tm
    tile_end = jnp.cumsum(tiles_per_expert)
    row_start = (tile_end - tiles_per_expert) * tm
    last_tile = jnp.maximum(tile_end - 1, 0)
    n_tiles = tile_end[-1:]
    tile = jnp.minimum(jnp.arange(n_tiles_max, dtype=jnp.int32), n_tiles - 1)
    tile_expert = jnp.sum((tile[:, None] >= tile_end[None, :]).astype(jnp.int32), axis=1)
    return row_start, last_tile, tile_expert, n_tiles


def _row(ref, r):
    return ref.at[pl.ds(pl.multiple_of(r * ROW_CHUNKS, ROW_CHUNKS), ROW_CHUNKS)]


def _dispatch_kernel(last_ref, nt_ref, pos_ref, y_ref, xs_ref, zero_sc, sem, *, ts, tm, n_tiles_max):
    def tile(i):
        return xs_ref.at[pl.ds(pl.multiple_of(i * (ROW_CHUNKS * tm), ROW_CHUNKS * tm), ROW_CHUNKS * tm)]

    @pl.when(pl.program_id(0) == 0)
    def _():
        zero_sc[...] = jnp.zeros(zero_sc.shape, F32)
        for e in range(N_EXPERTS):
            pltpu.make_async_copy(zero_sc, tile(last_ref[e]), sem).start()
        for e in range(N_EXPERTS):
            pltpu.make_async_copy(zero_sc, tile(last_ref[e]), sem).wait()

        def fill(i, carry):
            copy = pltpu.make_async_copy(zero_sc, tile(i), sem)
            copy.start()
            copy.wait()
            return carry

        lax.fori_loop(nt_ref[0], n_tiles_max, fill, 0)

    def issue(t, carry):
        for k in range(2):
            pltpu.make_async_copy(_row(y_ref, t), _row(xs_ref, pos_ref[0, k, t]), sem).start(priority=k)
        return carry

    lax.fori_loop(0, ts, issue, 0, unroll=8)
    for _ in range(2):
        pltpu.make_async_copy(y_ref, xs_ref.at[pl.ds(0, ROW_CHUNKS * ts)], sem).wait()


def _pos_spec(ts):
    return pl.BlockSpec((1, 2, ts), lambda i, *_: (i, 0, 0), memory_space=pltpu.SMEM)


def _dispatch(last_tile, n_tiles, pos, y_rows, n_rows, tm):
    ts = pos.shape[-1]
    s = y_rows.shape[0] // ROW_CHUNKS
    return pl.pallas_call(
        functools.partial(_dispatch_kernel, ts=ts, tm=tm, n_tiles_max=n_rows // tm),
        grid_spec=pltpu.PrefetchScalarGridSpec(
            num_scalar_prefetch=2,
            grid=(s // ts,),
            in_specs=[
                _pos_spec(ts),
                pl.BlockSpec((ROW_CHUNKS * ts, LANES), lambda i, *_: (i, 0)),
            ],
            out_specs=pl.BlockSpec(memory_space=pl.ANY),
            scratch_shapes=[pltpu.VMEM((ROW_CHUNKS * tm, LANES), F32), pltpu.SemaphoreType.DMA],
        ),
        out_shape=jax.ShapeDtypeStruct((ROW_CHUNKS * n_rows, LANES), F32),
        compiler_params=_params("arbitrary"),
        name="moe_dispatch",
    )(last_tile, n_tiles, pos, y_rows)


def _expert_kernel(te_ref, nt_ref, xs_ref, wg_ref, wu_ref, wd_ref, ys_ref, wg_b, wu_b, wd_b):
    i = pl.program_id(0)

    @pl.when(i < nt_ref[0])
    def _():
        @pl.when(jnp.logical_or(i == 0, te_ref[i] != te_ref[jnp.maximum(i - 1, 0)]))
        def _():
            wg_b[...] = wg_ref[0, 0].astype(BF16)
            wu_b[...] = wu_ref[0, 0].astype(BF16)
            wd_b[...] = wd_ref[0, 0].astype(BF16)

        x = _load_rows(xs_ref).astype(BF16)
        hg = _dot(x, wg_b[...])
        hu = _dot(x, wu_b[...])
        hidden = (hg * (1.0 / (1.0 + jnp.exp(-hg)))) * hu
        _store_rows(ys_ref, _dot(hidden.astype(BF16), wd_b[...]))

    @pl.when(i >= nt_ref[0])
    def _():
        ys_ref[...] = jnp.zeros(ys_ref.shape, ys_ref.dtype)


def _experts(tile_expert, n_tiles, xs, w_gate, w_up, w_down, layer, tm):
    n_rows = xs.shape[0] // ROW_CHUNKS
    block = (ROW_CHUNKS * tm, LANES)
    weight = lambda i, te, nt: (layer, te[i], 0, 0)
    return pl.pallas_call(
        _expert_kernel,
        grid_spec=pltpu.PrefetchScalarGridSpec(
            num_scalar_prefetch=2,
            grid=(n_rows // tm,),
            in_specs=[
                pl.BlockSpec(block, lambda i, te, nt: (jnp.minimum(i, nt[0] - 1), 0)),
                pl.BlockSpec((1, 1, D_MODEL, D_EXPERT), weight),
                pl.BlockSpec((1, 1, D_MODEL, D_EXPERT), weight),
                pl.BlockSpec((1, 1, D_EXPERT, D_MODEL), weight),
            ],
            out_specs=pl.BlockSpec(block, lambda i, te, nt: (i, 0)),
            scratch_shapes=[
                pltpu.VMEM((D_MODEL, D_EXPERT), BF16),
                pltpu.VMEM((D_MODEL, D_EXPERT), BF16),
                pltpu.VMEM((D_EXPERT, D_MODEL), BF16),
            ],
        ),
        out_shape=jax.ShapeDtypeStruct(xs.shape, F32),
        compiler_params=_params("arbitrary"),
        name="moe_experts",
    )(tile_expert, n_tiles, xs, w_gate, w_up, w_down)


def _combine_kernel(pos_ref, y_ref, rt_ref, ys_ref, g_ref, b_ref, out_ref, got1, got2, sem, *, ts):
    def issue(t, carry):
        for k, got in enumerate((got1, got2)):
            pltpu.make_async_copy(_row(ys_ref, pos_ref[0, k, t]), _row(got, t), sem).start(priority=k)
        return carry

    lax.fori_loop(0, ts, issue, 0, unroll=8)
    for got in (got1, got2):
        pltpu.make_async_copy(ys_ref.at[pl.ds(0, ROW_CHUNKS * ts)], got, sem).wait()
    rt = rt_ref[...]
    moe = rt[:, 2:3] * _load_rows(got1) + rt[:, 3:4] * _load_rows(got2)
    out_ref[...] = _layer_norm(DEEPNORM_ALPHA * _load_rows(y_ref) + moe, g_ref[...], b_ref[...])


def _combine(pos, y_rows, rt_cols, ys, g, b):
    ts = pos.shape[-1]
    s = y_rows.shape[0] // ROW_CHUNKS
    return pl.pallas_call(
        functools.partial(_combine_kernel, ts=ts),
        grid=(s // ts,),
        in_specs=[
            pl.BlockSpec((1, 2, ts), lambda i: (i, 0, 0), memory_space=pltpu.SMEM),
            pl.BlockSpec((ROW_CHUNKS * ts, LANES), lambda i: (i, 0)),
            pl.BlockSpec((ts, ROUTE_ROWS), lambda i: (i, 0)),
            pl.BlockSpec(memory_space=pl.ANY),
            _const_spec((1, D_MODEL)),
            _const_spec((1, D_MODEL)),
        ],
        out_specs=pl.BlockSpec((ts, D_MODEL), lambda i: (i, 0)),
        out_shape=jax.ShapeDtypeStruct((s, D_MODEL), F32),
        scratch_shapes=[
            pltpu.VMEM((ROW_CHUNKS * ts, LANES), F32),
            pltpu.VMEM((ROW_CHUNKS * ts, LANES), F32),
            pltpu.SemaphoreType.DMA,
        ],
        compiler_params=_params("arbitrary"),
        name="moe_combine_ln",
    )(pos, y_rows, rt_cols, ys, g, b)


def _moe(y_rows, rt, counts, w_gate, w_up, w_down, layer, g, b):
    s = rt.shape[1]
    ts = min(TOKEN_TILE, s)
    tm = MOE_TILE
    n_rows = 2 * s + N_EXPERTS * tm
    row_start, last_tile, tile_expert, n_tiles = _tile_plan(counts, tm, n_rows // tm)
    expert = rt[0:2].astype(jnp.int32)
    first_row = sum(jnp.where(expert == e, row_start[e], 0) for e in range(N_EXPERTS))
    pos = (first_row + rt[4:6].astype(jnp.int32)).reshape(2, s // ts, ts).transpose(1, 0, 2)
    xs = _dispatch(last_tile, n_tiles, pos, y_rows, n_rows, tm)
    ys = _experts(tile_expert, n_tiles, xs, w_gate, w_up, w_down, layer, tm)
    return _combine(pos, y_rows, rt.T, ys, g, b)


def _rope_lane_pad(t):
    pad = [(0, 0)] * (t.ndim - 1)
    return jnp.pad(t, pad + [(QK_NOPE, LANES - QK_NOPE - QK_ROPE)])


def _rotate_half(t):
    return jnp.concatenate([-t[..., HALF_ROPE:], t[..., :HALF_ROPE]], axis=-1)


def _prep_mla(w_dqkv, w_uq, w_ukv, w_o):
    w_kr = w_dqkv[:, Q_LORA + KV_LORA:]
    wd = jnp.concatenate(
        [w_dqkv[:, :Q_LORA + KV_LORA], _rope_lane_pad(w_kr), _rope_lane_pad(_rotate_half(w_kr))],
        axis=1).astype(BF16)
    uq = w_uq.reshape(Q_LORA, N_HEADS, QK_NOPE + QK_ROPE)
    zq = jnp.zeros((Q_LORA, N_HEADS, LANES - QK_NOPE - QK_ROPE), F32)
    q_plain = jnp.concatenate([uq, zq], axis=-1)
    q_rot = jnp.concatenate(
        [jnp.zeros((Q_LORA, N_HEADS, QK_NOPE), F32), _rotate_half(uq[..., QK_NOPE:]), zq], axis=-1)
    wq = jnp.concatenate([q_plain, q_rot], axis=-1).reshape(Q_LORA, N_HEADS * 2 * LANES).astype(BF16)
    ukv = w_ukv.reshape(KV_LORA, N_HEADS, QK_NOPE + V_HEAD)
    wk = jnp.pad(ukv[..., :QK_NOPE], ((0, 0), (0, 0), (0, LANES - QK_NOPE)))
    wk = wk.reshape(KV_LORA, N_HEADS * LANES).astype(BF16)
    wvt = jnp.pad(ukv[..., QK_NOPE:], ((0, 0), (0, 0), (0, VT_ROWS - V_HEAD)))
    wvt = wvt.reshape(KV_LORA, N_HEADS * VT_ROWS).T.astype(BF16)
    wo = jnp.pad(w_o.reshape(N_HEADS, V_HEAD, D_MODEL), ((0, 0), (0, LANES - V_HEAD), (0, 0)))
    wo = wo.reshape(N_HEADS * LANES, D_MODEL).astype(BF16)
    return wd, wq, wk, wvt, wo


def _prep_router(router_w, router_bias):
    hi = router_w.astype(BF16)
    lo = (router_w - hi.astype(F32)).astype(BF16)
    rw = jnp.concatenate([hi, lo, jnp.zeros((D_MODEL, LANES - 2 * N_EXPERTS), BF16)], axis=1)
    return rw, router_bias.reshape(N_EXPERTS, 1).astype(F32)


def kernel(x, positions, mla_w_dqkv, mla_q_norm, mla_kv_norm, mla_w_uq, mla_w_ukv, mla_w_o,
           conv_w_in, conv_kernel, conv_w_out, router_w, router_bias,
           moe_w_gate, moe_w_up, moe_w_down, ln_mix_g, ln_mix_b, ln_ffn_g, ln_ffn_b):
    batch, s, _ = x.shape
    rw, rbias = _prep_router(router_w, router_bias)
    inv_freq = ROPE_THETA ** (-jnp.arange(0, QK_ROPE, 2, dtype=F32) / QK_ROPE)
    invf = _rope_lane_pad(jnp.concatenate([inv_freq, inv_freq]))[None, :]
    ts = min(TOKEN_TILE, s)
    tri = jnp.triu(jnp.ones((ts, ts), BF16), k=1)
    row = lambda t: t.reshape(1, -1).astype(F32)

    outs = []
    for bi in range(batch):
        xc = x[bi]
        pos_col = positions[bi].reshape(s, 1)
        for i in range(DEPTH):
            j = i // 2
            if i % 2 == 0:
                wd, wq, wk, wvt, wo = _prep_mla(mla_w_dqkv[j], mla_w_uq[j], mla_w_ukv[j], mla_w_o[j])
                q, k, vt = _mla_proj(xc, pos_col, invf, wd, row(mla_q_norm[j]), row(mla_kv_norm[j]),
                                     wq, wk, wvt)
                o = _attention(q, k, vt)
                y, rt, cnt = _attn_out(xc, o, wo, row(ln_mix_g[i]), row(ln_mix_b[i]), rw, rbias, tri)
            else:
                y, rt, cnt = _conv_mixer(xc, conv_w_in[j].astype(BF16), conv_kernel[j].astype(F32),
                                         conv_w_out[j].astype(BF16), row(ln_mix_g[i]),
                                         row(ln_mix_b[i]), rw, rbias, tri)
            xc = _moe(y, rt, cnt, moe_w_gate, moe_w_up, moe_w_down, i, row(ln_ffn_g[i]), row(ln_ffn_b[i]))
        outs.append(xc)
    return jnp.stack(outs, axis=0)
```

```python
import functools

import jax
import jax.numpy as jnp
from jax import lax
from jax.experimental import pallas as pl
from jax.experimental.pallas import tpu as pltpu

D_MODEL = 1024
DEPTH = 4
N_HEADS = 16
QK_NOPE = 64
QK_ROPE = 32
V_HEAD = 64
Q_LORA = 256
KV_LORA = 128
ROPE_THETA = 10000.0
CONV_W = 3
N_EXPERTS = 16
N_GROUPS = 4
EXPERTS_PER_GROUP = N_EXPERTS // N_GROUPS
D_EXPERT = 512
DEEPNORM_ALPHA = float((2 * DEPTH) ** 0.25)
LN_EPS = 1e-5
RMS_EPS = 1e-6
QK_SCALE = (QK_NOPE + QK_ROPE) ** -0.5 * 1.4426950408889634

LANES = 128
SUBLANES = 8
VT_ROWS = 80
HALF_ROPE = QK_ROPE // 2
LAT_W = Q_LORA + KV_LORA + 2 * LANES
ROUTE_ROWS = 8
ROW_CHUNKS = D_MODEL // LANES
VMEM_LIMIT = 56 * 1024 * 1024

TOKEN_TILE = 512
ATTN_TILE = 512
MOE_TILE = 512

F32 = jnp.float32
BF16 = jnp.bfloat16
_NT = (((1,), (1,)), ((), ()))


def _params(*sem):
    return pltpu.CompilerParams(dimension_semantics=sem, vmem_limit_bytes=VMEM_LIMIT)


def _const_spec(shape):
    return pl.BlockSpec(shape, lambda *_: (0,) * len(shape))


def _dot(a, b):
    return jnp.dot(a, b, preferred_element_type=F32)


def _layer_norm(z, g, b):
    mu = jnp.mean(z, axis=-1, keepdims=True)
    zc = z - mu
    var = jnp.mean(zc * zc, axis=-1, keepdims=True)
    return zc * lax.rsqrt(var + LN_EPS) * g + b


def _rms_norm(z, g):
    return z * lax.rsqrt(jnp.mean(z * z, axis=-1, keepdims=True) + RMS_EPS) * g


def _mla_proj_kernel(x_ref, pos_ref, invf_ref, wd_ref, qn_ref, kvn_ref, wq_ref, wk_ref, wvt_ref,
                     q_out, k_out, vt_out):
    xb = x_ref[...].astype(BF16)
    lat = _dot(xb, wd_ref[...])
    cq = _rms_norm(lat[:, :Q_LORA], qn_ref[...]).astype(BF16)
    ckv = _rms_norm(lat[:, Q_LORA:Q_LORA + KV_LORA], kvn_ref[...]).astype(BF16)
    kr = lat[:, Q_LORA + KV_LORA:Q_LORA + KV_LORA + LANES]
    krr = lat[:, Q_LORA + KV_LORA + LANES:]
    ang = pos_ref[...].astype(F32) * invf_ref[...]
    cos = jnp.cos(ang)
    sin = jnp.sin(ang)
    k_rope = kr * cos + krr * sin
    ones_row = lax.broadcasted_iota(jnp.int32, (VT_ROWS, x_ref.shape[0]), 0) == V_HEAD
    for h in range(N_HEADS):
        qq = _dot(cq, wq_ref[:, h * 2 * LANES:(h + 1) * 2 * LANES])
        q_out[h] = ((qq[:, :LANES] * cos + qq[:, LANES:] * sin) * QK_SCALE).astype(BF16)
        k_out[h] = (_dot(ckv, wk_ref[:, h * LANES:(h + 1) * LANES]) + k_rope).astype(BF16)
        vt = lax.dot_general(wvt_ref[h * VT_ROWS:(h + 1) * VT_ROWS, :], ckv, _NT,
                             preferred_element_type=F32)
        vt_out[h, 0] = jnp.where(ones_row, 1.0, vt).astype(BF16)


def _mla_proj(x, pos_col, invf, wd, qn, kvn, wq, wk, wvt):
    s = x.shape[0]
    ts = min(ATTN_TILE, s)
    head_spec = pl.BlockSpec((N_HEADS, ts, LANES), lambda i: (0, i, 0))
    out_sds = jax.ShapeDtypeStruct((N_HEADS, s, LANES), BF16)
    return pl.pallas_call(
        _mla_proj_kernel,
        grid=(s // ts,),
        in_specs=[
            pl.BlockSpec((ts, D_MODEL), lambda i: (i, 0)),
            pl.BlockSpec((ts, 1), lambda i: (i, 0)),
            _const_spec((1, LANES)),
            _const_spec((D_MODEL, LAT_W)),
            _const_spec((1, Q_LORA)),
            _const_spec((1, KV_LORA)),
            _const_spec((Q_LORA, N_HEADS * 2 * LANES)),
            _const_spec((KV_LORA, N_HEADS * LANES)),
            _const_spec((N_HEADS * VT_ROWS, KV_LORA)),
        ],
        out_specs=[head_spec, head_spec,
                   pl.BlockSpec((N_HEADS, 1, VT_ROWS, ts), lambda i: (0, i, 0, 0))],
        out_shape=[out_sds, out_sds,
                   jax.ShapeDtypeStruct((N_HEADS, s // ts, VT_ROWS, ts), BF16)],
        compiler_params=_params("parallel"),
        name="mla_proj",
    )(x, pos_col, invf, wd, qn, kvn, wq, wk, wvt)


def _sublane_allmax(part):
    for shift in (4, 2, 1):
        part = jnp.maximum(part, pltpu.roll(part, shift, 0))
    return part


def _attn_kernel(q_ref, k_ref, vt_ref, o_ref, s_a, s_b, mx_a, mx_b, acc_sc, m_sc, *, t):
    qi = pl.program_id(1)
    q = q_ref[0]
    groups = t // SUBLANES
    m_sc[...] = jnp.full(m_sc.shape, -jnp.inf, F32)
    acc_sc[...] = jnp.zeros(acc_sc.shape, F32)

    def scores(j, s_dst, mx_dst, diagonal):
        k = k_ref[0, pl.ds(pl.multiple_of(j * t, t), t), :]
        s = lax.dot_general(k, q, _NT, preferred_element_type=F32)
        if diagonal:
            key = lax.broadcasted_iota(jnp.int32, s.shape, 0)
            qry = lax.broadcasted_iota(jnp.int32, s.shape, 1)
            s = jnp.where(key <= qry, s, -jnp.inf)
        s_dst[...] = s
        mx_dst[...] = jnp.max(s.reshape(groups, SUBLANES, t), axis=0)

    def softmax_pv(j, s_src, mx_src):
        m_prev = m_sc[...]
        m_new = jnp.maximum(m_prev, _sublane_allmax(mx_src[...]))
        alpha = jnp.exp2(m_prev - m_new)
        p = jnp.exp2(s_src[...].reshape(groups, SUBLANES, t) - m_new[None])
        pv = _dot(vt_ref[0, j], p.reshape(t, t).astype(BF16))
        acc = acc_sc[...].reshape(VT_ROWS // SUBLANES, SUBLANES, t) * alpha[None]
        acc_sc[...] = acc.reshape(VT_ROWS, t) + pv
        m_sc[...] = m_new

    scores(0, s_a, mx_a, False)

    @pl.when(qi == 0)
    def _():
        scores(0, s_a, mx_a, True)

    def pair(i, carry):
        j = 2 * i
        scores(j + 1, s_b, mx_b, False)
        softmax_pv(j, s_a, mx_a)
        scores(j + 2, s_a, mx_a, False)
        softmax_pv(j + 1, s_b, mx_b)
        return carry

    def quad(i, carry):
        pair(2 * i, carry)
        return pair(2 * i + 1, carry)

    def octet(i, carry):
        quad(2 * i, carry)
        return quad(2 * i + 1, carry)

    n_pairs = jnp.maximum((qi - 1) // 2, 0)
    n_quads = n_pairs // 2
    n_octets = n_quads // 2
    lax.fori_loop(0, n_octets, octet, 0)
    lax.fori_loop(2 * n_octets, n_quads, quad, 0)
    lax.fori_loop(2 * n_quads, n_pairs, pair, 0)
    j0 = 2 * n_pairs

    @pl.when(jnp.logical_and(qi > 0, qi % 2 == 1))
    def _():
        scores(j0 + 1, s_b, mx_b, True)
        softmax_pv(j0, s_a, mx_a)
        softmax_pv(j0 + 1, s_b, mx_b)

    @pl.when(jnp.logical_and(qi > 0, qi % 2 == 0))
    def _():
        scores(j0 + 1, s_b, mx_b, False)
        softmax_pv(j0, s_a, mx_a)
        scores(j0 + 2, s_a, mx_a, True)
        softmax_pv(j0 + 1, s_b, mx_b)
        softmax_pv(j0 + 2, s_a, mx_a)

    @pl.when(qi == 0)
    def _():
        softmax_pv(0, s_a, mx_a)

    acc = acc_sc[...]
    o_t = acc[:V_HEAD] / acc[V_HEAD:V_HEAD + 1, :]
    o_t = jnp.concatenate([o_t, jnp.zeros((LANES - V_HEAD, t), F32)], axis=0)
    o_ref[...] = o_t.T.astype(o_ref.dtype)


def _attention(q, k, vt):
    s = q.shape[1]
    t = vt.shape[-1]
    return pl.pallas_call(
        functools.partial(_attn_kernel, t=t),
        grid=(N_HEADS, s // t),
        in_specs=[
            pl.BlockSpec((1, t, LANES), lambda h, i: (h, i, 0)),
            pl.BlockSpec((1, s, LANES), lambda h, i: (h, 0, 0)),
            pl.BlockSpec((1, s // t, VT_ROWS, t), lambda h, i: (h, 0, 0, 0)),
        ],
        out_specs=pl.BlockSpec((t, LANES), lambda h, i: (i, h)),
        out_shape=jax.ShapeDtypeStruct((s, N_HEADS * LANES), BF16),
        scratch_shapes=[
            pltpu.VMEM((t, t), F32),
            pltpu.VMEM((t, t), F32),
            pltpu.VMEM((SUBLANES, t), F32),
            pltpu.VMEM((SUBLANES, t), F32),
            pltpu.VMEM((VT_ROWS, t), F32),
            pltpu.VMEM((SUBLANES, t), F32),
        ],
        compiler_params=_params("parallel", "parallel"),
        name="mla_attention",
    )(q, k, vt)


def _top2_of_4(a0, a1, a2, a3):
    hi1, lo1 = jnp.maximum(a0, a1), jnp.minimum(a0, a1)
    hi2, lo2 = jnp.maximum(a2, a3), jnp.minimum(a2, a3)
    first = jnp.maximum(hi1, hi2)
    second = jnp.maximum(jnp.minimum(hi1, hi2), jnp.maximum(lo1, lo2))
    return first + second


def _argmax_first(vals):
    best, idx = vals[0], jnp.zeros_like(vals[0])
    for i in range(1, len(vals)):
        take = vals[i] > best
        best = jnp.where(take, vals[i], best)
        idx = jnp.where(take, float(i), idx)
    return idx, best


def _pick(idx, rows):
    out = rows[-1]
    for i in range(len(rows) - 2, -1, -1):
        out = jnp.where(idx == float(i), rows[i], out)
    return out


def _store_rows(ref, v):
    n = v.shape[0]
    for c in range(ROW_CHUNKS):
        ref[pl.ds(c, n, stride=ROW_CHUNKS), :] = v[:, c * LANES:(c + 1) * LANES]


def _load_rows(ref):
    n = ref.shape[0] // ROW_CHUNKS
    return jnp.concatenate(
        [ref[pl.ds(c, n, stride=ROW_CHUNKS), :] for c in range(ROW_CHUNKS)], axis=1)


def _route_rows(y, rw_ref, rbias_ref, tri_ref, run_sc):
    y_hi = y.astype(BF16)
    y_lo = (y - y_hi.astype(F32)).astype(BF16)
    prod = _dot(y_hi, rw_ref[...]) + _dot(y_lo, rw_ref[...])
    prod_t = prod.T
    logits = prod_t[:N_EXPERTS] + prod_t[N_EXPERTS:2 * N_EXPERTS]
    scores = 1.0 / (1.0 + jnp.exp(-logits))
    biased = scores + rbias_ref[...]
    b = [biased[e:e + 1] for e in range(N_EXPERTS)]
    sc = [scores[e:e + 1] for e in range(N_EXPERTS)]
    group_scores = [_top2_of_4(*b[g * 4:g * 4 + 4]) for g in range(N_GROUPS)]
    g_idx, _ = _argmax_first(group_scores)
    in_b = [_pick(g_idx, [b[g * 4 + j] for g in range(N_GROUPS)]) for j in range(EXPERTS_PER_GROUP)]
    in_s = [_pick(g_idx, [sc[g * 4 + j] for g in range(N_GROUPS)]) for j in range(EXPERTS_PER_GROUP)]
    l1, _ = _argmax_first(in_b)
    rest = [jnp.where(l1 == float(j), -jnp.inf, in_b[j]) for j in range(EXPERTS_PER_GROUP)]
    l2, _ = _argmax_first(rest)
    w1 = _pick(l1, in_s)
    w2 = _pick(l2, in_s)
    wsum = w1 + w2
    e1 = g_idx * float(EXPERTS_PER_GROUP) + l1
    e2 = g_idx * float(EXPERTS_PER_GROUP) + l2
    eid = lax.broadcasted_iota(jnp.int32, logits.shape, 0).astype(F32)
    hit1 = eid == e1
    hit2 = eid == e2
    member = jnp.where(jnp.logical_or(hit1, hit2), 1.0, 0.0)
    run = run_sc[...]
    before = _dot(member.astype(BF16), tri_ref[...]) + run[:, 0:1]
    r1 = jnp.sum(jnp.where(hit1, before, 0.0), axis=0, keepdims=True)
    r2 = jnp.sum(jnp.where(hit2, before, 0.0), axis=0, keepdims=True)
    run_sc[...] = run + jnp.sum(member, axis=1, keepdims=True)
    zero = jnp.zeros_like(e1)
    return jnp.concatenate([e1, e2, w1 / wsum, w2 / wsum, r1, r2, zero, zero], axis=0)


def _mixer_epilogue(x, h, g_ref, b_ref, rw_ref, rbias_ref, tri_ref, y_out, rt_out, cnt_out, run_sc):
    @pl.when(pl.program_id(0) == 0)
    def _():
        run_sc[...] = jnp.zeros(run_sc.shape, F32)

    y = _layer_norm(DEEPNORM_ALPHA * x + h, g_ref[...], b_ref[...])
    _store_rows(y_out, y)
    rt_out[...] = _route_rows(y, rw_ref, rbias_ref, tri_ref, run_sc)
    cnt_out[...] = run_sc[...]


def _epilogue_specs(ts):
    in_specs = [
        _const_spec((1, D_MODEL)),
        _const_spec((1, D_MODEL)),
        _const_spec((D_MODEL, LANES)),
        _const_spec((N_EXPERTS, 1)),
        _const_spec((ts, ts)),
    ]
    out_specs = [
        pl.BlockSpec((ROW_CHUNKS * ts, LANES), lambda i: (i, 0)),
        pl.BlockSpec((ROUTE_ROWS, ts), lambda i: (0, i)),
        _const_spec((N_EXPERTS, LANES)),
    ]
    return in_specs, out_specs


def _epilogue_shapes(s):
    return [
        jax.ShapeDtypeStruct((ROW_CHUNKS * s, LANES), F32),
        jax.ShapeDtypeStruct((ROUTE_ROWS, s), F32),
        jax.ShapeDtypeStruct((N_EXPERTS, LANES), F32),
    ]


_RUN_SCRATCH = pltpu.VMEM((N_EXPERTS, LANES), F32)


def _attn_out_kernel(x_ref, o_ref, wo_ref, *epilogue_refs):
    h = _dot(o_ref[...], wo_ref[...])
    _mixer_epilogue(x_ref[...], h, *epilogue_refs)


def _attn_out(x, o, wo, g, b, rw, rbias, tri):
    s = x.shape[0]
    ts = min(TOKEN_TILE, s)
    ep_in, ep_out = _epilogue_specs(ts)
    return pl.pallas_call(
        _attn_out_kernel,
        grid=(s // ts,),
        in_specs=[
            pl.BlockSpec((ts, D_MODEL), lambda i: (i, 0)),
            pl.BlockSpec((ts, N_HEADS * LANES), lambda i: (i, 0)),
            _const_spec((N_HEADS * LANES, D_MODEL)),
        ] + ep_in,
        out_specs=ep_out,
        out_shape=_epilogue_shapes(s),
        scratch_shapes=[_RUN_SCRATCH],
        compiler_params=_params("arbitrary"),
        name="attn_out_ln_route",
    )(x, o, wo, g, b, rw, rbias, tri)


def _conv_kernel(x_ref, win_ref, ck_ref, wout_ref, g_ref, b_ref, rw_ref, rbias_ref, tri_ref,
                 y_out, rt_out, cnt_out, u_sc, run_sc, *, ts):
    @pl.when(pl.program_id(0) == 0)
    def _():
        u_sc[0:8, :] = jnp.zeros((8, D_MODEL), F32)

    x = x_ref[...]
    xb = x.astype(BF16)
    c_gate = _dot(xb, win_ref[:, D_MODEL:2 * D_MODEL])
    u = c_gate * _dot(xb, win_ref[:, 2 * D_MODEL:])
    u_sc[8:8 + ts, :] = u
    ck = ck_ref[...]
    conv = u_sc[6:6 + ts, :] * ck[0:1] + u_sc[7:7 + ts, :] * ck[1:2] + u * ck[2:3]
    u_sc[0:8, :] = u_sc[ts:ts + 8, :]
    b_gate = _dot(xb, win_ref[:, :D_MODEL])
    h = _dot((b_gate * conv).astype(BF16), wout_ref[...])
    _mixer_epilogue(x, h, g_ref, b_ref, rw_ref, rbias_ref, tri_ref, y_out, rt_out, cnt_out, run_sc)


def _conv_mixer(x, win, ck, wout, g, b, rw, rbias, tri):
    s = x.shape[0]
    ts = min(TOKEN_TILE, s)
    ep_in, ep_out = _epilogue_specs(ts)
    return pl.pallas_call(
        functools.partial(_conv_kernel, ts=ts),
        grid=(s // ts,),
        in_specs=[
            pl.BlockSpec((ts, D_MODEL), lambda i: (i, 0)),
            _const_spec((D_MODEL, 3 * D_MODEL)),
            _const_spec((CONV_W, D_MODEL)),
            _const_spec((D_MODEL, D_MODEL)),
        ] + ep_in,
        out_specs=ep_out,
        out_shape=_epilogue_shapes(s),
        scratch_shapes=[pltpu.VMEM((ts + 8, D_MODEL), F32), _RUN_SCRATCH],
        compiler_params=_params("arbitrary"),
        name="conv_mixer_ln_route",
    )(x, win, ck, wout, g, b, rw, rbias, tri)


def _tile_plan(counts, tm, n_tiles_max):
    counts = counts[:, 0].astype(jnp.int32)
    tiles_per_expert = (counts + tm - 1) // tm
    tile_end = jnp.cumsum(tiles_per_expert)
    row_start = (tile_end - tiles_per_expert) * tm
    last_tile = jnp.maximum(tile_end - 1, 0)
    n_tiles = tile_end[-1:]
    tile = jnp.minimum(jnp.arange(n_tiles_max, dtype=jnp.int32), n_tiles - 1)
    tile_expert = jnp.sum((tile[:, None] >= tile_end[None, :]).astype(jnp.int32), axis=1)
    return row_start, last_tile, tile_expert, n_tiles


def _row(ref, r):
    return ref.at[pl.ds(pl.multiple_of(r * ROW_CHUNKS, ROW_CHUNKS), ROW_CHUNKS)]


def _dispatch_kernel(last_ref, nt_ref, pos_ref, y_ref, xs_ref, zero_sc, sem, *, ts, tm, n_tiles_max):
    def tile(i):
        return xs_ref.at[pl.ds(pl.multiple_of(i * (ROW_CHUNKS * tm), ROW_CHUNKS * tm), ROW_CHUNKS * tm)]

    @pl.when(pl.program_id(0) == 0)
    def _():
        zero_sc[...] = jnp.zeros(zero_sc.shape, F32)
        for e in range(N_EXPERTS):
            pltpu.make_async_copy(zero_sc, tile(last_ref[e]), sem).start()
        for e in range(N_EXPERTS):
            pltpu.make_async_copy(zero_sc, tile(last_ref[e]), sem).wait()

        def fill(i, carry):
            copy = pltpu.make_async_copy(zero_sc, tile(i), sem)
            copy.start()
            copy.wait()
            return carry

        lax.fori_loop(nt_ref[0], n_tiles_max, fill, 0)

    def issue(t, carry):
        for k in range(2):
            pltpu.make_async_copy(_row(y_ref, t), _row(xs_ref, pos_ref[0, k, t]), sem).start(priority=k)
        return carry

    lax.fori_loop(0, ts, issue, 0, unroll=8)
    for _ in range(2):
        pltpu.make_async_copy(y_ref, xs_ref.at[pl.ds(0, ROW_CHUNKS * ts)], sem).wait()


def _pos_spec(ts):
    return pl.BlockSpec((1, 2, ts), lambda i, *_: (i, 0, 0), memory_space=pltpu.SMEM)


def _dispatch(last_tile, n_tiles, pos, y_rows, n_rows, tm):
    ts = pos.shape[-1]
    s = y_rows.shape[0] // ROW_CHUNKS
    return pl.pallas_call(
        functools.partial(_dispatch_kernel, ts=ts, tm=tm, n_tiles_max=n_rows // tm),
        grid_spec=pltpu.PrefetchScalarGridSpec(
            num_scalar_prefetch=2,
            grid=(s // ts,),
            in_specs=[
                _pos_spec(ts),
                pl.BlockSpec((ROW_CHUNKS * ts, LANES), lambda i, *_: (i, 0)),
            ],
            out_specs=pl.BlockSpec(memory_space=pl.ANY),
            scratch_shapes=[pltpu.VMEM((ROW_CHUNKS * tm, LANES), F32), pltpu.SemaphoreType.DMA],
        ),
        out_shape=jax.ShapeDtypeStruct((ROW_CHUNKS * n_rows, LANES), F32),
        compiler_params=_params("arbitrary"),
        name="moe_dispatch",
    )(last_tile, n_tiles, pos, y_rows)


def _expert_kernel(te_ref, nt_ref, xs_ref, wg_ref, wu_ref, wd_ref, ys_ref, wg_b, wu_b, wd_b):
    i = pl.program_id(0)

    @pl.when(i < nt_ref[0])
    def _():
        @pl.when(jnp.logical_or(i == 0, te_ref[i] != te_ref[jnp.maximum(i - 1, 0)]))
        def _():
            wg_b[...] = wg_ref[0, 0].astype(BF16)
            wu_b[...] = wu_ref[0, 0].astype(BF16)
            wd_b[...] = wd_ref[0, 0].astype(BF16)

        x = _load_rows(xs_ref).astype(BF16)
        hg = _dot(x, wg_b[...])
        hu = _dot(x, wu_b[...])
        hidden = (hg * (1.0 / (1.0 + jnp.exp(-hg)))) * hu
        _store_rows(ys_ref, _dot(hidden.astype(BF16), wd_b[...]))

    @pl.when(i >= nt_ref[0])
    def _():
        ys_ref[...] = jnp.zeros(ys_ref.shape, ys_ref.dtype)


def _experts(tile_expert, n_tiles, xs, w_gate, w_up, w_down, layer, tm):
    n_rows = xs.shape[0] // ROW_CHUNKS
    block = (ROW_CHUNKS * tm, LANES)
    weight = lambda i, te, nt: (layer, te[i], 0, 0)
    return pl.pallas_call(
        _expert_kernel,
        grid_spec=pltpu.PrefetchScalarGridSpec(
            num_scalar_prefetch=2,
            grid=(n_rows // tm,),
            in_specs=[
                pl.BlockSpec(block, lambda i, te, nt: (jnp.minimum(i, nt[0] - 1), 0)),
                pl.BlockSpec((1, 1, D_MODEL, D_EXPERT), weight),
                pl.BlockSpec((1, 1, D_MODEL, D_EXPERT), weight),
                pl.BlockSpec((1, 1, D_EXPERT, D_MODEL), weight),
            ],
            out_specs=pl.BlockSpec(block, lambda i, te, nt: (i, 0)),
            scratch_shapes=[
                pltpu.VMEM((D_MODEL, D_EXPERT), BF16),
                pltpu.VMEM((D_MODEL, D_EXPERT), BF16),
                pltpu.VMEM((D_EXPERT, D_MODEL), BF16),
            ],
        ),
        out_shape=jax.ShapeDtypeStruct(xs.shape, F32),
        compiler_params=_params("arbitrary"),
        name="moe_experts",
    )(tile_expert, n_tiles, xs, w_gate, w_up, w_down)


def _combine_kernel(pos_ref, y_ref, rt_ref, ys_ref, g_ref, b_ref, out_ref, got1, got2, sem, *, ts):
    def issue(t, carry):
        for k, got in enumerate((got1, got2)):
            pltpu.make_async_copy(_row(ys_ref, pos_ref[0, k, t]), _row(got, t), sem).start(priority=k)
        return carry

    lax.fori_loop(0, ts, issue, 0, unroll=8)
    for got in (got1, got2):
        pltpu.make_async_copy(ys_ref.at[pl.ds(0, ROW_CHUNKS * ts)], got, sem).wait()
    rt = rt_ref[...]
    moe = rt[:, 2:3] * _load_rows(got1) + rt[:, 3:4] * _load_rows(got2)
    out_ref[...] = _layer_norm(DEEPNORM_ALPHA * _load_rows(y_ref) + moe, g_ref[...], b_ref[...])


def _combine(pos, y_rows, rt_cols, ys, g, b):
    ts = pos.shape[-1]
    s = y_rows.shape[0] // ROW_CHUNKS
    return pl.pallas_call(
        functools.partial(_combine_kernel, ts=ts),
        grid=(s // ts,),
        in_specs=[
            pl.BlockSpec((1, 2, ts), lambda i: (i, 0, 0), memory_space=pltpu.SMEM),
            pl.BlockSpec((ROW_CHUNKS * ts, LANES), lambda i: (i, 0)),
            pl.BlockSpec((ts, ROUTE_ROWS), lambda i: (i, 0)),
            pl.BlockSpec(memory_space=pl.ANY),
            _const_spec((1, D_MODEL)),
            _const_spec((1, D_MODEL)),
        ],
        out_specs=pl.BlockSpec((ts, D_MODEL), lambda i: (i, 0)),
        out_shape=jax.ShapeDtypeStruct((s, D_MODEL), F32),
        scratch_shapes=[
            pltpu.VMEM((ROW_CHUNKS * ts, LANES), F32),
            pltpu.VMEM((ROW_CHUNKS * ts, LANES), F32),
            pltpu.SemaphoreType.DMA,
        ],
        compiler_params=_params("arbitrary"),
        name="moe_combine_ln",
    )(pos, y_rows, rt_cols, ys, g, b)


def _moe(y_rows, rt, counts, w_gate, w_up, w_down, layer, g, b):
    s = rt.shape[1]
    ts = min(TOKEN_TILE, s)
    tm = MOE_TILE
    n_rows = 2 * s + N_EXPERTS * tm
    row_start, last_tile, tile_expert, n_tiles = _tile_plan(counts, tm, n_rows // tm)
    expert = rt[0:2].astype(jnp.int32)
    first_row = sum(jnp.where(expert == e, row_start[e], 0) for e in range(N_EXPERTS))
    pos = (first_row + rt[4:6].astype(jnp.int32)).reshape(2, s // ts, ts).transpose(1, 0, 2)
    xs = _dispatch(last_tile, n_tiles, pos, y_rows, n_rows, tm)
    ys = _experts(tile_expert, n_tiles, xs, w_gate, w_up, w_down, layer, tm)
    return _combine(pos, y_rows, rt.T, ys, g, b)


def _rope_lane_pad(t):
    pad = [(0, 0)] * (t.ndim - 1)
    return jnp.pad(t, pad + [(QK_NOPE, LANES - QK_NOPE - QK_ROPE)])


def _rotate_half(t):
    return jnp.concatenate([-t[..., HALF_ROPE:], t[..., :HALF_ROPE]], axis=-1)


def _prep_mla(w_dqkv, w_uq, w_ukv, w_o):
    w_kr = w_dqkv[:, Q_LORA + KV_LORA:]
    wd = jnp.concatenate(
        [w_dqkv[:, :Q_LORA + KV_LORA], _rope_lane_pad(w_kr), _rope_lane_pad(_rotate_half(w_kr))],
        axis=1).astype(BF16)
    uq = w_uq.reshape(Q_LORA, N_HEADS, QK_NOPE + QK_ROPE)
    zq = jnp.zeros((Q_LORA, N_HEADS, LANES - QK_NOPE - QK_ROPE), F32)
    q_plain = jnp.concatenate([uq, zq], axis=-1)
    q_rot = jnp.concatenate(
        [jnp.zeros((Q_LORA, N_HEADS, QK_NOPE), F32), _rotate_half(uq[..., QK_NOPE:]), zq], axis=-1)
    wq = jnp.concatenate([q_plain, q_rot], axis=-1).reshape(Q_LORA, N_HEADS * 2 * LANES).astype(BF16)
    ukv = w_ukv.reshape(KV_LORA, N_HEADS, QK_NOPE + V_HEAD)
    wk = jnp.pad(ukv[..., :QK_NOPE], ((0, 0), (0, 0), (0, LANES - QK_NOPE)))
    wk = wk.reshape(KV_LORA, N_HEADS * LANES).astype(BF16)
    wvt = jnp.pad(ukv[..., QK_NOPE:], ((0, 0), (0, 0), (0, VT_ROWS - V_HEAD)))
    wvt = wvt.reshape(KV_LORA, N_HEADS * VT_ROWS).T.astype(BF16)
    wo = jnp.pad(w_o.reshape(N_HEADS, V_HEAD, D_MODEL), ((0, 0), (0, LANES - V_HEAD), (0, 0)))
    wo = wo.reshape(N_HEADS * LANES, D_MODEL).astype(BF16)
    return wd, wq, wk, wvt, wo


def _prep_router(router_w, router_bias):
    hi = router_w.astype(BF16)
    lo = (router_w - hi.astype(F32)).astype(BF16)
    rw = jnp.concatenate([hi, lo, jnp.zeros((D_MODEL, LANES - 2 * N_EXPERTS), BF16)], axis=1)
    return rw, router_bias.reshape(N_EXPERTS, 1).astype(F32)


def kernel(x, positions, mla_w_dqkv, mla_q_norm, mla_kv_norm, mla_w_uq, mla_w_ukv, mla_w_o,
           conv_w_in, conv_kernel, conv_w_out, router_w, router_bias,
           moe_w_gate, moe_w_up, moe_w_down, ln_mix_g, ln_mix_b, ln_ffn_g, ln_ffn_b):
    batch, s, _ = x.shape
    rw, rbias = _prep_router(router_w, router_bias)
    inv_freq = ROPE_THETA ** (-jnp.arange(0, QK_ROPE, 2, dtype=F32) / QK_ROPE)
    invf = _rope_lane_pad(jnp.concatenate([inv_freq, inv_freq]))[None, :]
    ts = min(TOKEN_TILE, s)
    tri = jnp.triu(jnp.ones((ts, ts), BF16), k=1)
    row = lambda t: t.reshape(1, -1).astype(F32)

    outs = []
    for bi in range(batch):
        xc = x[bi]
        pos_col = positions[bi].reshape(s, 1)
        for i in range(DEPTH):
            j = i // 2
            if i % 2 == 0:
                wd, wq, wk, wvt, wo = _prep_mla(mla_w_dqkv[j], mla_w_uq[j], mla_w_ukv[j], mla_w_o[j])
                q, k, vt = _mla_proj(xc, pos_col, invf, wd, row(mla_q_norm[j]), row(mla_kv_norm[j]),
                                     wq, wk, wvt)
                o = _attention(q, k, vt)
                y, rt, cnt = _attn_out(xc, o, wo, row(ln_mix_g[i]), row(ln_mix_b[i]), rw, rbias, tri)
            else:
                y, rt, cnt = _conv_mixer(xc, conv_w_in[j].astype(BF16), conv_kernel[j].astype(F32),
                                         conv_w_out[j].astype(BF16), row(ln_mix_g[i]),
                                         row(ln_mix_b[i]), rw, rbias, tri)
            xc = _moe(y, rt, cnt, moe_w_gate, moe_w_up, moe_w_down, i, row(ln_ffn_g[i]), row(ln_ffn_b[i]))
        outs.append(xc)
    return jnp.stack(outs, axis=0)
```

```python
import functools

import jax
import jax.numpy as jnp
from jax import lax
from jax.experimental import pallas as pl
from jax.experimental.pallas import tpu as pltpu

D_MODEL = 1024
DEPTH = 4
N_HEADS = 16
QK_NOPE = 64
QK_ROPE = 32
V_HEAD = 64
Q_LORA = 256
KV_LORA = 128
ROPE_THETA = 10000.0
CONV_W = 3
N_EXPERTS = 16
N_GROUPS = 4
EXPERTS_PER_GROUP = N_EXPERTS // N_GROUPS
D_EXPERT = 512
DEEPNORM_ALPHA = float((2 * DEPTH) ** 0.25)
LN_EPS = 1e-5
RMS_EPS = 1e-6
QK_SCALE = (QK_NOPE + QK_ROPE) ** -0.5 * 1.4426950408889634

LANES = 128
SUBLANES = 8
VT_ROWS = 80
HALF_ROPE = QK_ROPE // 2
LAT_W = Q_LORA + KV_LORA + 2 * LANES
ROUTE_ROWS = 8
ROW_CHUNKS = D_MODEL // LANES
VMEM_LIMIT = 56 * 1024 * 1024

TOKEN_TILE = 512
ATTN_TILE = 512
MOE_TILE = 512
MOVE_TILE = 1024

F32 = jnp.float32
BF16 = jnp.bfloat16
_NT = (((1,), (1,)), ((), ()))


def _params(*sem):
    return pltpu.CompilerParams(dimension_semantics=sem, vmem_limit_bytes=VMEM_LIMIT)


def _const_spec(shape):
    return pl.BlockSpec(shape, lambda *_: (0,) * len(shape))


def _dot(a, b):
    return jnp.dot(a, b, preferred_element_type=F32)


def _layer_norm(z, g, b):
    mu = jnp.mean(z, axis=-1, keepdims=True)
    zc = z - mu
    var = jnp.mean(zc * zc, axis=-1, keepdims=True)
    return zc * lax.rsqrt(var + LN_EPS) * g + b


def _rms_norm(z, g):
    return z * lax.rsqrt(jnp.mean(z * z, axis=-1, keepdims=True) + RMS_EPS) * g


def _mla_proj_kernel(x_ref, pos_ref, invf_ref, wd_ref, qn_ref, kvn_ref, wq_ref, wk_ref, wvt_ref,
                     q_out, k_out, vt_out):
    xb = x_ref[...].astype(BF16)
    lat = _dot(xb, wd_ref[...])
    cq = _rms_norm(lat[:, :Q_LORA], qn_ref[...]).astype(BF16)
    ckv = _rms_norm(lat[:, Q_LORA:Q_LORA + KV_LORA], kvn_ref[...]).astype(BF16)
    kr = lat[:, Q_LORA + KV_LORA:Q_LORA + KV_LORA + LANES]
    krr = lat[:, Q_LORA + KV_LORA + LANES:]
    ang = pos_ref[...].astype(F32) * invf_ref[...]
    cos = jnp.cos(ang)
    sin = jnp.sin(ang)
    k_rope = kr * cos + krr * sin
    ones_row = lax.broadcasted_iota(jnp.int32, (VT_ROWS, x_ref.shape[0]), 0) == V_HEAD
    for h in range(N_HEADS):
        qq = _dot(cq, wq_ref[:, h * 2 * LANES:(h + 1) * 2 * LANES])
        q_out[h] = ((qq[:, :LANES] * cos + qq[:, LANES:] * sin) * QK_SCALE).astype(BF16)
        k_out[h] = (_dot(ckv, wk_ref[:, h * LANES:(h + 1) * LANES]) + k_rope).astype(BF16)
        vt = lax.dot_general(wvt_ref[h * VT_ROWS:(h + 1) * VT_ROWS, :], ckv, _NT,
                             preferred_element_type=F32)
        vt_out[h, 0] = jnp.where(ones_row, 1.0, vt).astype(BF16)


def _mla_proj(x, pos_col, invf, wd, qn, kvn, wq, wk, wvt):
    s = x.shape[0]
    ts = min(ATTN_TILE, s)
    head_spec = pl.BlockSpec((N_HEADS, ts, LANES), lambda i: (0, i, 0))
    out_sds = jax.ShapeDtypeStruct((N_HEADS, s, LANES), BF16)
    return pl.pallas_call(
        _mla_proj_kernel,
        grid=(s // ts,),
        in_specs=[
            pl.BlockSpec((ts, D_MODEL), lambda i: (i, 0)),
            pl.BlockSpec((ts, 1), lambda i: (i, 0)),
            _const_spec((1, LANES)),
            _const_spec((D_MODEL, LAT_W)),
            _const_spec((1, Q_LORA)),
            _const_spec((1, KV_LORA)),
            _const_spec((Q_LORA, N_HEADS * 2 * LANES)),
            _const_spec((KV_LORA, N_HEADS * LANES)),
            _const_spec((N_HEADS * VT_ROWS, KV_LORA)),
        ],
        out_specs=[head_spec, head_spec,
                   pl.BlockSpec((N_HEADS, 1, VT_ROWS, ts), lambda i: (0, i, 0, 0))],
        out_shape=[out_sds, out_sds,
                   jax.ShapeDtypeStruct((N_HEADS, s // ts, VT_ROWS, ts), BF16)],
        compiler_params=_params("parallel"),
        name="mla_proj",
    )(x, pos_col, invf, wd, qn, kvn, wq, wk, wvt)


def _sublane_allmax(part):
    for shift in (4, 2, 1):
        part = jnp.maximum(part, pltpu.roll(part, shift, 0))
    return part


def _attn_kernel(q_ref, k_ref, vt_ref, o_ref, s_a, s_b, mx_a, mx_b, acc_sc, m_sc, *, t):
    qi = pl.program_id(1)
    q = q_ref[0]
    groups = t // SUBLANES
    m_sc[...] = jnp.full(m_sc.shape, -jnp.inf, F32)
    acc_sc[...] = jnp.zeros(acc_sc.shape, F32)

    def scores(j, s_dst, mx_dst, diagonal):
        k = k_ref[0, pl.ds(pl.multiple_of(j * t, t), t), :]
        s = lax.dot_general(k, q, _NT, preferred_element_type=F32)
        if diagonal:
            key = lax.broadcasted_iota(jnp.int32, s.shape, 0)
            qry = lax.broadcasted_iota(jnp.int32, s.shape, 1)
            s = jnp.where(key <= qry, s, -jnp.inf)
        s_dst[...] = s
        mx_dst[...] = jnp.max(s.reshape(groups, SUBLANES, t), axis=0)

    def softmax_pv(j, s_src, mx_src):
        m_prev = m_sc[...]
        m_new = jnp.maximum(m_prev, _sublane_allmax(mx_src[...]))
        alpha = jnp.exp2(m_prev - m_new)
        p = jnp.exp2(s_src[...].reshape(groups, SUBLANES, t) - m_new[None])
        pv = _dot(vt_ref[0, j], p.reshape(t, t).astype(BF16))
        acc = acc_sc[...].reshape(VT_ROWS // SUBLANES, SUBLANES, t) * alpha[None]
        acc_sc[...] = acc.reshape(VT_ROWS, t) + pv
        m_sc[...] = m_new

    scores(0, s_a, mx_a, False)

    @pl.when(qi == 0)
    def _():
        scores(0, s_a, mx_a, True)

    def pair(i, carry):
        j = 2 * i
        scores(j + 1, s_b, mx_b, False)
        softmax_pv(j, s_a, mx_a)
        scores(j + 2, s_a, mx_a, False)
        softmax_pv(j + 1, s_b, mx_b)
        return carry

    def quad(i, carry):
        pair(2 * i, carry)
        return pair(2 * i + 1, carry)

    def octet(i, carry):
        quad(2 * i, carry)
        return quad(2 * i + 1, carry)

    n_pairs = jnp.maximum((qi - 1) // 2, 0)
    n_quads = n_pairs // 2
    n_octets = n_quads // 2
    lax.fori_loop(0, n_octets, octet, 0)
    lax.fori_loop(2 * n_octets, n_quads, quad, 0)
    lax.fori_loop(2 * n_quads, n_pairs, pair, 0)
    j0 = 2 * n_pairs

    @pl.when(jnp.logical_and(qi > 0, qi % 2 == 1))
    def _():
        scores(j0 + 1, s_b, mx_b, True)
        softmax_pv(j0, s_a, mx_a)
        softmax_pv(j0 + 1, s_b, mx_b)

    @pl.when(jnp.logical_and(qi > 0, qi % 2 == 0))
    def _():
        scores(j0 + 1, s_b, mx_b, False)
        softmax_pv(j0, s_a, mx_a)
        scores(j0 + 2, s_a, mx_a, True)
        softmax_pv(j0 + 1, s_b, mx_b)
        softmax_pv(j0 + 2, s_a, mx_a)

    @pl.when(qi == 0)
    def _():
        softmax_pv(0, s_a, mx_a)

    acc = acc_sc[...]
    o_ref[...] = (acc[:V_HEAD] / acc[V_HEAD:V_HEAD + 1, :]).astype(o_ref.dtype)


def _attention(q, k, vt):
    s = q.shape[1]
    t = vt.shape[-1]
    return pl.pallas_call(
        functools.partial(_attn_kernel, t=t),
        grid=(N_HEADS, s // t),
        in_specs=[
            pl.BlockSpec((1, t, LANES), lambda h, i: (h, i, 0)),
            pl.BlockSpec((1, s, LANES), lambda h, i: (h, 0, 0)),
            pl.BlockSpec((1, s // t, VT_ROWS, t), lambda h, i: (h, 0, 0, 0)),
        ],
        out_specs=pl.BlockSpec((V_HEAD, t), lambda h, i: (h, i)),
        out_shape=jax.ShapeDtypeStruct((N_HEADS * V_HEAD, s), BF16),
        scratch_shapes=[
            pltpu.VMEM((t, t), F32),
            pltpu.VMEM((t, t), F32),
            pltpu.VMEM((SUBLANES, t), F32),
            pltpu.VMEM((SUBLANES, t), F32),
            pltpu.VMEM((VT_ROWS, t), F32),
            pltpu.VMEM((SUBLANES, t), F32),
        ],
        compiler_params=_params("parallel", "parallel"),
        name="mla_attention",
    )(q, k, vt)


def _top2_of_4(a0, a1, a2, a3):
    hi1, lo1 = jnp.maximum(a0, a1), jnp.minimum(a0, a1)
    hi2, lo2 = jnp.maximum(a2, a3), jnp.minimum(a2, a3)
    first = jnp.maximum(hi1, hi2)
    second = jnp.maximum(jnp.minimum(hi1, hi2), jnp.maximum(lo1, lo2))
    return first + second


def _argmax_first(vals):
    best, idx = vals[0], jnp.zeros_like(vals[0])
    for i in range(1, len(vals)):
        take = vals[i] > best
        best = jnp.where(take, vals[i], best)
        idx = jnp.where(take, float(i), idx)
    return idx, best


def _pick(idx, rows):
    out = rows[-1]
    for i in range(len(rows) - 2, -1, -1):
        out = jnp.where(idx == float(i), rows[i], out)
    return out


def _store_rows(ref, v):
    n = v.shape[0]
    for c in range(ROW_CHUNKS):
        ref[pl.ds(c, n, stride=ROW_CHUNKS), :] = v[:, c * LANES:(c + 1) * LANES]


def _load_rows(ref):
    n = ref.shape[0] // ROW_CHUNKS
    return jnp.concatenate(
        [ref[pl.ds(c, n, stride=ROW_CHUNKS), :] for c in range(ROW_CHUNKS)], axis=1)


def _route_rows(y, rw_ref, rbias_ref, tri_ref, run_sc):
    y_hi = y.astype(BF16)
    y_lo = (y - y_hi.astype(F32)).astype(BF16)
    prod = _dot(y_hi, rw_ref[...]) + _dot(y_lo, rw_ref[...])
    prod_t = prod.T
    logits = prod_t[:N_EXPERTS] + prod_t[N_EXPERTS:2 * N_EXPERTS]
    scores = 1.0 / (1.0 + jnp.exp(-logits))
    biased = scores + rbias_ref[...]
    b = [biased[e:e + 1] for e in range(N_EXPERTS)]
    sc = [scores[e:e + 1] for e in range(N_EXPERTS)]
    group_scores = [_top2_of_4(*b[g * 4:g * 4 + 4]) for g in range(N_GROUPS)]
    g_idx, _ = _argmax_first(group_scores)
    in_b = [_pick(g_idx, [b[g * 4 + j] for g in range(N_GROUPS)]) for j in range(EXPERTS_PER_GROUP)]
    in_s = [_pick(g_idx, [sc[g * 4 + j] for g in range(N_GROUPS)]) for j in range(EXPERTS_PER_GROUP)]
    l1, _ = _argmax_first(in_b)
    rest = [jnp.where(l1 == float(j), -jnp.inf, in_b[j]) for j in range(EXPERTS_PER_GROUP)]
    l2, _ = _argmax_first(rest)
    w1 = _pick(l1, in_s)
    w2 = _pick(l2, in_s)
    wsum = w1 + w2
    e1 = g_idx * float(EXPERTS_PER_GROUP) + l1
    e2 = g_idx * float(EXPERTS_PER_GROUP) + l2
    eid = lax.broadcasted_iota(jnp.int32, logits.shape, 0).astype(F32)
    hit1 = eid == e1
    hit2 = eid == e2
    member = jnp.where(jnp.logical_or(hit1, hit2), 1.0, 0.0)
    run = run_sc[...]
    before = _dot(member.astype(BF16), tri_ref[...]) + run[:, 0:1]
    r1 = jnp.sum(jnp.where(hit1, before, 0.0), axis=0, keepdims=True)
    r2 = jnp.sum(jnp.where(hit2, before, 0.0), axis=0, keepdims=True)
    run_sc[...] = run + jnp.sum(member, axis=1, keepdims=True)
    zero = jnp.zeros_like(e1)
    return jnp.concatenate([e1, e2, w1 / wsum, w2 / wsum, r1, r2, zero, zero], axis=0)


def _mixer_epilogue(x, h, g_ref, b_ref, rw_ref, rbias_ref, tri_ref, y_out, rt_out, cnt_out, run_sc):
    @pl.when(pl.program_id(0) == 0)
    def _():
        run_sc[...] = jnp.zeros(run_sc.shape, F32)

    y = _layer_norm(DEEPNORM_ALPHA * x + h, g_ref[...], b_ref[...])
    _store_rows(y_out, y)
    rt_out[...] = _route_rows(y, rw_ref, rbias_ref, tri_ref, run_sc)
    cnt_out[...] = run_sc[...]


def _epilogue_specs(ts):
    in_specs = [
        _const_spec((1, D_MODEL)),
        _const_spec((1, D_MODEL)),
        _const_spec((D_MODEL, LANES)),
        _const_spec((N_EXPERTS, 1)),
        _const_spec((ts, ts)),
    ]
    out_specs = [
        pl.BlockSpec((ROW_CHUNKS * ts, LANES), lambda i: (i, 0)),
        pl.BlockSpec((ROUTE_ROWS, ts), lambda i: (0, i)),
        _const_spec((N_EXPERTS, LANES)),
    ]
    return in_specs, out_specs


def _epilogue_shapes(s):
    return [
        jax.ShapeDtypeStruct((ROW_CHUNKS * s, LANES), F32),
        jax.ShapeDtypeStruct((ROUTE_ROWS, s), F32),
        jax.ShapeDtypeStruct((N_EXPERTS, LANES), F32),
    ]


_RUN_SCRATCH = pltpu.VMEM((N_EXPERTS, LANES), F32)


def _attn_out_kernel(x_ref, o_ref, wo_ref, *epilogue_refs):
    h = lax.dot_general(o_ref[...], wo_ref[...], (((0,), (0,)), ((), ())), preferred_element_type=F32)
    _mixer_epilogue(x_ref[...], h, *epilogue_refs)


def _attn_out(x, o, wo, g, b, rw, rbias, tri):
    s = x.shape[0]
    ts = min(TOKEN_TILE, s)
    ep_in, ep_out = _epilogue_specs(ts)
    return pl.pallas_call(
        _attn_out_kernel,
        grid=(s // ts,),
        in_specs=[
            pl.BlockSpec((ts, D_MODEL), lambda i: (i, 0)),
            pl.BlockSpec((N_HEADS * V_HEAD, ts), lambda i: (0, i)),
            _const_spec((N_HEADS * V_HEAD, D_MODEL)),
        ] + ep_in,
        out_specs=ep_out,
        out_shape=_epilogue_shapes(s),
        scratch_shapes=[_RUN_SCRATCH],
        compiler_params=_params("arbitrary"),
        name="attn_out_ln_route",
    )(x, o, wo, g, b, rw, rbias, tri)


def _conv_kernel(x_ref, win_ref, ck_ref, wout_ref, g_ref, b_ref, rw_ref, rbias_ref, tri_ref,
                 y_out, rt_out, cnt_out, u_sc, run_sc, *, ts):
    @pl.when(pl.program_id(0) == 0)
    def _():
        u_sc[0:8, :] = jnp.zeros((8, D_MODEL), F32)

    x = x_ref[...]
    xb = x.astype(BF16)
    c_gate = _dot(xb, win_ref[:, D_MODEL:2 * D_MODEL])
    u = c_gate * _dot(xb, win_ref[:, 2 * D_MODEL:])
    u_sc[8:8 + ts, :] = u
    ck = ck_ref[...]
    conv = u_sc[6:6 + ts, :] * ck[0:1] + u_sc[7:7 + ts, :] * ck[1:2] + u * ck[2:3]
    u_sc[0:8, :] = u_sc[ts:ts + 8, :]
    b_gate = _dot(xb, win_ref[:, :D_MODEL])
    h = _dot((b_gate * conv).astype(BF16), wout_ref[...])
    _mixer_epilogue(x, h, g_ref, b_ref, rw_ref, rbias_ref, tri_ref, y_out, rt_out, cnt_out, run_sc)


def _conv_mixer(x, win, ck, wout, g, b, rw, rbias, tri):
    s = x.shape[0]
    ts = min(TOKEN_TILE, s)
    ep_in, ep_out = _epilogue_specs(ts)
    return pl.pallas_call(
        functools.partial(_conv_kernel, ts=ts),
        grid=(s // ts,),
        in_specs=[
            pl.BlockSpec((ts, D_MODEL), lambda i: (i, 0)),
            _const_spec((D_MODEL, 3 * D_MODEL)),
            _const_spec((CONV_W, D_MODEL)),
            _const_spec((D_MODEL, D_MODEL)),
        ] + ep_in,
        out_specs=ep_out,
        out_shape=_epilogue_shapes(s),
        scratch_shapes=[pltpu.VMEM((ts + 8, D_MODEL), F32), _RUN_SCRATCH],
        compiler_params=_params("arbitrary"),
        name="conv_mixer_ln_route",
    )(x, win, ck, wout, g, b, rw, rbias, tri)


def _tile_plan(counts, tm, n_tiles_max):
    counts = counts[:, 0].astype(jnp.int32)
    tiles_per_expert = (counts + tm - 1) // tm
    tile_end = jnp.cumsum(tiles_per_expert)
    row_start = (tile_end - tiles_per_expert) * tm
    last_tile = jnp.maximum(tile_end - 1, 0)
    n_tiles = tile_end[-1:]
    tile = jnp.minimum(jnp.arange(n_tiles_max, dtype=jnp.int32), n_tiles - 1)
    tile_expert = jnp.sum((tile[:, None] >= tile_end[None, :]).astype(jnp.int32), axis=1)
    return row_start, last_tile, tile_expert, n_tiles


def _row(ref, r):
    return ref.at[pl.ds(pl.multiple_of(r * ROW_CHUNKS, ROW_CHUNKS), ROW_CHUNKS)]


def _dispatch_kernel(last_ref, nt_ref, pos_ref, y_ref, xs_ref, zero_sc, sem, *, ts, tm, n_tiles_max):
    def tile(i):
        return xs_ref.at[pl.ds(pl.multiple_of(i * (ROW_CHUNKS * tm), ROW_CHUNKS * tm), ROW_CHUNKS * tm)]

    @pl.when(pl.program_id(0) == 0)
    def _():
        zero_sc[...] = jnp.zeros(zero_sc.shape, F32)
        for e in range(N_EXPERTS):
            pltpu.make_async_copy(zero_sc, tile(last_ref[e]), sem).start()
        for e in range(N_EXPERTS):
            pltpu.make_async_copy(zero_sc, tile(last_ref[e]), sem).wait()

        def fill(i, carry):
            copy = pltpu.make_async_copy(zero_sc, tile(i), sem)
            copy.start()
            copy.wait()
            return carry

        lax.fori_loop(nt_ref[0], n_tiles_max, fill, 0)

    def issue(t, carry):
        for k in range(2):
            pltpu.make_async_copy(_row(y_ref, t), _row(xs_ref, pos_ref[0, k, t]), sem).start(priority=k)
        return carry

    lax.fori_loop(0, ts, issue, 0, unroll=8)
    for _ in range(2):
        pltpu.make_async_copy(y_ref, xs_ref.at[pl.ds(0, ROW_CHUNKS * ts)], sem).wait()


def _pos_spec(ts):
    return pl.BlockSpec((1, 2, ts), lambda i, *_: (i, 0, 0), memory_space=pltpu.SMEM)


def _dispatch(last_tile, n_tiles, pos, y_rows, n_rows, tm):
    ts = pos.shape[-1]
    s = y_rows.shape[0] // ROW_CHUNKS
    return pl.pallas_call(
        functools.partial(_dispatch_kernel, ts=ts, tm=tm, n_tiles_max=n_rows // tm),
        grid_spec=pltpu.PrefetchScalarGridSpec(
            num_scalar_prefetch=2,
            grid=(s // ts,),
            in_specs=[
                _pos_spec(ts),
                pl.BlockSpec((ROW_CHUNKS * ts, LANES), lambda i, *_: (i, 0)),
            ],
            out_specs=pl.BlockSpec(memory_space=pl.ANY),
            scratch_shapes=[pltpu.VMEM((ROW_CHUNKS * tm, LANES), F32), pltpu.SemaphoreType.DMA],
        ),
        out_shape=jax.ShapeDtypeStruct((ROW_CHUNKS * n_rows, LANES), F32),
        compiler_params=_params("arbitrary"),
        name="moe_dispatch",
    )(last_tile, n_tiles, pos, y_rows)


def _expert_kernel(te_ref, nt_ref, xs_ref, wg_ref, wu_ref, wd_ref, ys_ref, wg_b, wu_b, wd_b):
    i = pl.program_id(0)

    @pl.when(i < nt_ref[0])
    def _():
        @pl.when(jnp.logical_or(i == 0, te_ref[i] != te_ref[jnp.maximum(i - 1, 0)]))
        def _():
            wg_b[...] = wg_ref[0, 0].astype(BF16)
            wu_b[...] = wu_ref[0, 0].astype(BF16)
            wd_b[...] = wd_ref[0, 0].astype(BF16)

        x = _load_rows(xs_ref).astype(BF16)
        hg = _dot(x, wg_b[...])
        hu = _dot(x, wu_b[...])
        hidden = (hg * (1.0 / (1.0 + jnp.exp(-hg)))) * hu
        _store_rows(ys_ref, _dot(hidden.astype(BF16), wd_b[...]))

    @pl.when(i >= nt_ref[0])
    def _():
        ys_ref[...] = jnp.zeros(ys_ref.shape, ys_ref.dtype)


def _experts(tile_expert, n_tiles, xs, w_gate, w_up, w_down, layer, tm):
    n_rows = xs.shape[0] // ROW_CHUNKS
    block = (ROW_CHUNKS * tm, LANES)
    weight = lambda i, te, nt: (layer, te[i], 0, 0)
    return pl.pallas_call(
        _expert_kernel,
        grid_spec=pltpu.PrefetchScalarGridSpec(
            num_scalar_prefetch=2,
            grid=(n_rows // tm,),
            in_specs=[
                pl.BlockSpec(block, lambda i, te, nt: (jnp.minimum(i, nt[0] - 1), 0)),
                pl.BlockSpec((1, 1, D_MODEL, D_EXPERT), weight),
                pl.BlockSpec((1, 1, D_MODEL, D_EXPERT), weight),
                pl.BlockSpec((1, 1, D_EXPERT, D_MODEL), weight),
            ],
            out_specs=pl.BlockSpec(block, lambda i, te, nt: (i, 0)),
            scratch_shapes=[
                pltpu.VMEM((D_MODEL, D_EXPERT), BF16),
                pltpu.VMEM((D_MODEL, D_EXPERT), BF16),
                pltpu.VMEM((D_EXPERT, D_MODEL), BF16),
            ],
        ),
        out_shape=jax.ShapeDtypeStruct(xs.shape, F32),
        compiler_params=_params("arbitrary"),
        name="moe_experts",
    )(tile_expert, n_tiles, xs, w_gate, w_up, w_down)


def _combine_kernel(pos_ref, y_ref, rt_ref, ys_ref, g_ref, b_ref, out_ref, got1, got2, sem, *, ts):
    def issue(t, carry):
        for k, got in enumerate((got1, got2)):
            pltpu.make_async_copy(_row(ys_ref, pos_ref[0, k, t]), _row(got, t), sem).start(priority=k)
        return carry

    lax.fori_loop(0, ts, issue, 0, unroll=8)
    for got in (got1, got2):
        pltpu.make_async_copy(ys_ref.at[pl.ds(0, ROW_CHUNKS * ts)], got, sem).wait()
    rt = rt_ref[...]
    moe = rt[:, 2:3] * _load_rows(got1) + rt[:, 3:4] * _load_rows(got2)
    out_ref[...] = _layer_norm(DEEPNORM_ALPHA * _load_rows(y_ref) + moe, g_ref[...], b_ref[...])


def _combine(pos, y_rows, rt_cols, ys, g, b):
    ts = pos.shape[-1]
    s = y_rows.shape[0] // ROW_CHUNKS
    return pl.pallas_call(
        functools.partial(_combine_kernel, ts=ts),
        grid=(s // ts,),
        in_specs=[
            pl.BlockSpec((1, 2, ts), lambda i: (i, 0, 0), memory_space=pltpu.SMEM),
            pl.BlockSpec((ROW_CHUNKS * ts, LANES), lambda i: (i, 0)),
            pl.BlockSpec((ts, ROUTE_ROWS), lambda i: (i, 0)),
            pl.BlockSpec(memory_space=pl.ANY),
            _const_spec((1, D_MODEL)),
            _const_spec((1, D_MODEL)),
        ],
        out_specs=pl.BlockSpec((ts, D_MODEL), lambda i: (i, 0)),
        out_shape=jax.ShapeDtypeStruct((s, D_MODEL), F32),
        scratch_shapes=[
            pltpu.VMEM((ROW_CHUNKS * ts, LANES), F32),
            pltpu.VMEM((ROW_CHUNKS * ts, LANES), F32),
            pltpu.SemaphoreType.DMA,
        ],
        compiler_params=_params("arbitrary"),
        name="moe_combine_ln",
    )(pos, y_rows, rt_cols, ys, g, b)


def _moe(y_rows, rt, counts, w_gate, w_up, w_down, layer, g, b):
    s = rt.shape[1]
    ts = min(MOVE_TILE, s)
    tm = MOE_TILE
    n_rows = 2 * s + N_EXPERTS * tm
    row_start, last_tile, tile_expert, n_tiles = _tile_plan(counts, tm, n_rows // tm)
    expert = rt[0:2].astype(jnp.int32)
    first_row = sum(jnp.where(expert == e, row_start[e], 0) for e in range(N_EXPERTS))
    pos = (first_row + rt[4:6].astype(jnp.int32)).reshape(2, s // ts, ts).transpose(1, 0, 2)
    xs = _dispatch(last_tile, n_tiles, pos, y_rows, n_rows, tm)
    ys = _experts(tile_expert, n_tiles, xs, w_gate, w_up, w_down, layer, tm)
    return _combine(pos, y_rows, rt.T, ys, g, b)


def _rope_lane_pad(t):
    pad = [(0, 0)] * (t.ndim - 1)
    return jnp.pad(t, pad + [(QK_NOPE, LANES - QK_NOPE - QK_ROPE)])


def _rotate_half(t):
    return jnp.concatenate([-t[..., HALF_ROPE:], t[..., :HALF_ROPE]], axis=-1)


def _prep_mla(w_dqkv, w_uq, w_ukv, w_o):
    w_kr = w_dqkv[:, Q_LORA + KV_LORA:]
    wd = jnp.concatenate(
        [w_dqkv[:, :Q_LORA + KV_LORA], _rope_lane_pad(w_kr), _rope_lane_pad(_rotate_half(w_kr))],
        axis=1).astype(BF16)
    uq = w_uq.reshape(Q_LORA, N_HEADS, QK_NOPE + QK_ROPE)
    zq = jnp.zeros((Q_LORA, N_HEADS, LANES - QK_NOPE - QK_ROPE), F32)
    q_plain = jnp.concatenate([uq, zq], axis=-1)
    q_rot = jnp.concatenate(
        [jnp.zeros((Q_LORA, N_HEADS, QK_NOPE), F32), _rotate_half(uq[..., QK_NOPE:]), zq], axis=-1)
    wq = jnp.concatenate([q_plain, q_rot], axis=-1).reshape(Q_LORA, N_HEADS * 2 * LANES).astype(BF16)
    ukv = w_ukv.reshape(KV_LORA, N_HEADS, QK_NOPE + V_HEAD)
    wk = jnp.pad(ukv[..., :QK_NOPE], ((0, 0), (0, 0), (0, LANES - QK_NOPE)))
    wk = wk.reshape(KV_LORA, N_HEADS * LANES).astype(BF16)
    wvt = jnp.pad(ukv[..., QK_NOPE:], ((0, 0), (0, 0), (0, VT_ROWS - V_HEAD)))
    wvt = wvt.reshape(KV_LORA, N_HEADS * VT_ROWS).T.astype(BF16)
    wo = w_o.astype(BF16)
    return wd, wq, wk, wvt, wo


def _prep_router(router_w, router_bias):
    hi = router_w.astype(BF16)
    lo = (router_w - hi.astype(F32)).astype(BF16)
    rw = jnp.concatenate([hi, lo, jnp.zeros((D_MODEL, LANES - 2 * N_EXPERTS), BF16)], axis=1)
    return rw, router_bias.reshape(N_EXPERTS, 1).astype(F32)


def kernel(x, positions, mla_w_dqkv, mla_q_norm, mla_kv_norm, mla_w_uq, mla_w_ukv, mla_w_o,
           conv_w_in, conv_kernel, conv_w_out, router_w, router_bias,
           moe_w_gate, moe_w_up, moe_w_down, ln_mix_g, ln_mix_b, ln_ffn_g, ln_ffn_b):
    batch, s, _ = x.shape
    rw, rbias = _prep_router(router_w, router_bias)
    inv_freq = ROPE_THETA ** (-jnp.arange(0, QK_ROPE, 2, dtype=F32) / QK_ROPE)
    invf = _rope_lane_pad(jnp.concatenate([inv_freq, inv_freq]))[None, :]
    ts = min(TOKEN_TILE, s)
    tri = jnp.triu(jnp.ones((ts, ts), BF16), k=1)
    row = lambda t: t.reshape(1, -1).astype(F32)

    outs = []
    for bi in range(batch):
        xc = x[bi]
        pos_col = positions[bi].reshape(s, 1)
        for i in range(DEPTH):
            j = i // 2
            if i % 2 == 0:
                wd, wq, wk, wvt, wo = _prep_mla(mla_w_dqkv[j], mla_w_uq[j], mla_w_ukv[j], mla_w_o[j])
                q, k, vt = _mla_proj(xc, pos_col, invf, wd, row(mla_q_norm[j]), row(mla_kv_norm[j]),
                                     wq, wk, wvt)
                o = _attention(q, k, vt)
                y, rt, cnt = _attn_out(xc, o, wo, row(ln_mix_g[i]), row(ln_mix_b[i]), rw, rbias, tri)
            else:
                y, rt, cnt = _conv_mixer(xc, conv_w_in[j].astype(BF16), conv_kernel[j].astype(F32),
                                         conv_w_out[j].astype(BF16), row(ln_mix_g[i]),
                                         row(ln_mix_b[i]), rw, rbias, tri)
            xc = _moe(y, rt, cnt, moe_w_gate, moe_w_up, moe_w_down, i, row(ln_ffn_g[i]), row(ln_ffn_b[i]))
        outs.append(xc)
    return jnp.stack(outs, axis=0)
```

```python
import functools

import jax
import jax.numpy as jnp
from jax import lax
from jax.experimental import pallas as pl
from jax.experimental.pallas import tpu as pltpu

D_MODEL = 1024
DEPTH = 4
N_HEADS = 16
QK_NOPE = 64
QK_ROPE = 32
V_HEAD = 64
Q_LORA = 256
KV_LORA = 128
ROPE_THETA = 10000.0
CONV_W = 3
N_EXPERTS = 16
N_GROUPS = 4
EXPERTS_PER_GROUP = N_EXPERTS // N_GROUPS
D_EXPERT = 512
DEEPNORM_ALPHA = float((2 * DEPTH) ** 0.25)
LN_EPS = 1e-5
RMS_EPS = 1e-6
QK_SCALE = (QK_NOPE + QK_ROPE) ** -0.5 * 1.4426950408889634

LANES = 128
SUBLANES = 8
VT_ROWS = 80
HALF_ROPE = QK_ROPE // 2
LAT_W = Q_LORA + KV_LORA + 2 * LANES
ROUTE_ROWS = 8
ROW_CHUNKS = D_MODEL // LANES
VMEM_LIMIT = 56 * 1024 * 1024

TOKEN_TILE = 512
ATTN_TILE = 512
MOE_TILE = 512
MOVE_TILE = 1024
HEADS_PER_STEP = 2

F32 = jnp.float32
BF16 = jnp.bfloat16
_NT = (((1,), (1,)), ((), ()))


def _params(*sem):
    return pltpu.CompilerParams(dimension_semantics=sem, vmem_limit_bytes=VMEM_LIMIT)


def _const_spec(shape):
    return pl.BlockSpec(shape, lambda *_: (0,) * len(shape))


def _dot(a, b):
    return jnp.dot(a, b, preferred_element_type=F32)


def _layer_norm(z, g, b):
    mu = jnp.mean(z, axis=-1, keepdims=True)
    zc = z - mu
    var = jnp.mean(zc * zc, axis=-1, keepdims=True)
    return zc * lax.rsqrt(var + LN_EPS) * g + b


def _rms_norm(z, g):
    return z * lax.rsqrt(jnp.mean(z * z, axis=-1, keepdims=True) + RMS_EPS) * g


def _mla_proj_kernel(x_ref, pos_ref, invf_ref, wd_ref, qn_ref, kvn_ref, wq_ref, wk_ref, wvt_ref,
                     q_out, k_out, vt_out):
    xb = x_ref[...].astype(BF16)
    lat = _dot(xb, wd_ref[...])
    cq = _rms_norm(lat[:, :Q_LORA], qn_ref[...]).astype(BF16)
    ckv = _rms_norm(lat[:, Q_LORA:Q_LORA + KV_LORA], kvn_ref[...]).astype(BF16)
    kr = lat[:, Q_LORA + KV_LORA:Q_LORA + KV_LORA + LANES]
    krr = lat[:, Q_LORA + KV_LORA + LANES:]
    ang = pos_ref[...].astype(F32) * invf_ref[...]
    cos = jnp.cos(ang)
    sin = jnp.sin(ang)
    k_rope = kr * cos + krr * sin
    ones_row = lax.broadcasted_iota(jnp.int32, (VT_ROWS, x_ref.shape[0]), 0) == V_HEAD
    for h in range(N_HEADS):
        qq = _dot(cq, wq_ref[:, h * 2 * LANES:(h + 1) * 2 * LANES])
        q_out[h] = ((qq[:, :LANES] * cos + qq[:, LANES:] * sin) * QK_SCALE).astype(BF16)
        k_out[h] = (_dot(ckv, wk_ref[:, h * LANES:(h + 1) * LANES]) + k_rope).astype(BF16)
        vt = lax.dot_general(wvt_ref[h * VT_ROWS:(h + 1) * VT_ROWS, :], ckv, _NT,
                             preferred_element_type=F32)
        vt_out[h, 0] = jnp.where(ones_row, 1.0, vt).astype(BF16)


def _mla_proj(x, pos_col, invf, wd, qn, kvn, wq, wk, wvt):
    s = x.shape[0]
    ts = min(ATTN_TILE, s)
    head_spec = pl.BlockSpec((N_HEADS, ts, LANES), lambda i: (0, i, 0))
    out_sds = jax.ShapeDtypeStruct((N_HEADS, s, LANES), BF16)
    return pl.pallas_call(
        _mla_proj_kernel,
        grid=(s // ts,),
        in_specs=[
            pl.BlockSpec((ts, D_MODEL), lambda i: (i, 0)),
            pl.BlockSpec((ts, 1), lambda i: (i, 0)),
            _const_spec((1, LANES)),
            _const_spec((D_MODEL, LAT_W)),
            _const_spec((1, Q_LORA)),
            _const_spec((1, KV_LORA)),
            _const_spec((Q_LORA, N_HEADS * 2 * LANES)),
            _const_spec((KV_LORA, N_HEADS * LANES)),
            _const_spec((N_HEADS * VT_ROWS, KV_LORA)),
        ],
        out_specs=[head_spec, head_spec,
                   pl.BlockSpec((N_HEADS, 1, VT_ROWS, ts), lambda i: (0, i, 0, 0))],
        out_shape=[out_sds, out_sds,
                   jax.ShapeDtypeStruct((N_HEADS, s // ts, VT_ROWS, ts), BF16)],
        compiler_params=_params("parallel"),
        name="mla_proj",
    )(x, pos_col, invf, wd, qn, kvn, wq, wk, wvt)


def _sublane_allmax(part):
    for shift in (4, 2, 1):
        part = jnp.maximum(part, pltpu.roll(part, shift, 0))
    return part


def _attn_kernel(q_ref, k_ref, vt_ref, o_ref, s_a, s_b, mx_a, mx_b, acc_sc, m_sc, *, t):
    qi = pl.program_id(1)
    groups = t // SUBLANES

    def one_head(hh, carry_out):
        q = q_ref[hh]
        m_sc[...] = jnp.full(m_sc.shape, -jnp.inf, F32)
        acc_sc[...] = jnp.zeros(acc_sc.shape, F32)

        def scores(j, s_dst, mx_dst, diagonal):
            k = k_ref[hh, pl.ds(pl.multiple_of(j * t, t), t), :]
            s = lax.dot_general(k, q, _NT, preferred_element_type=F32)
            if diagonal:
                key = lax.broadcasted_iota(jnp.int32, s.shape, 0)
                qry = lax.broadcasted_iota(jnp.int32, s.shape, 1)
                s = jnp.where(key <= qry, s, -jnp.inf)
            s_dst[...] = s
            mx_dst[...] = jnp.max(s.reshape(groups, SUBLANES, t), axis=0)

        def softmax_pv(j, s_src, mx_src):
            m_prev = m_sc[...]
            m_new = jnp.maximum(m_prev, _sublane_allmax(mx_src[...]))
            alpha = jnp.exp2(m_prev - m_new)
            p = jnp.exp2(s_src[...].reshape(groups, SUBLANES, t) - m_new[None])
            pv = _dot(vt_ref[hh, j], p.reshape(t, t).astype(BF16))
            acc = acc_sc[...].reshape(VT_ROWS // SUBLANES, SUBLANES, t) * alpha[None]
            acc_sc[...] = acc.reshape(VT_ROWS, t) + pv
            m_sc[...] = m_new

        scores(0, s_a, mx_a, False)

        @pl.when(qi == 0)
        def _():
            scores(0, s_a, mx_a, True)

        def pair(i, carry):
            j = 2 * i
            scores(j + 1, s_b, mx_b, False)
            softmax_pv(j, s_a, mx_a)
            scores(j + 2, s_a, mx_a, False)
            softmax_pv(j + 1, s_b, mx_b)
            return carry

        def quad(i, carry):
            pair(2 * i, carry)
            return pair(2 * i + 1, carry)

        def octet(i, carry):
            quad(2 * i, carry)
            return quad(2 * i + 1, carry)

        n_pairs = jnp.maximum((qi - 1) // 2, 0)
        n_quads = n_pairs // 2
        n_octets = n_quads // 2
        lax.fori_loop(0, n_octets, octet, 0)
        lax.fori_loop(2 * n_octets, n_quads, quad, 0)
        lax.fori_loop(2 * n_quads, n_pairs, pair, 0)
        j0 = 2 * n_pairs

        @pl.when(jnp.logical_and(qi > 0, qi % 2 == 1))
        def _():
            scores(j0 + 1, s_b, mx_b, True)
            softmax_pv(j0, s_a, mx_a)
            softmax_pv(j0 + 1, s_b, mx_b)

        @pl.when(jnp.logical_and(qi > 0, qi % 2 == 0))
        def _():
            scores(j0 + 1, s_b, mx_b, False)
            softmax_pv(j0, s_a, mx_a)
            scores(j0 + 2, s_a, mx_a, True)
            softmax_pv(j0 + 1, s_b, mx_b)
            softmax_pv(j0 + 2, s_a, mx_a)

        @pl.when(qi == 0)
        def _():
            softmax_pv(0, s_a, mx_a)

        acc = acc_sc[...]
        o = acc[:V_HEAD] / acc[V_HEAD:V_HEAD + 1, :]
        o_ref[pl.ds(pl.multiple_of(hh * V_HEAD, V_HEAD), V_HEAD), :] = o.astype(o_ref.dtype)
        return carry_out

    lax.fori_loop(0, HEADS_PER_STEP, one_head, 0)


def _attention(q, k, vt):
    s = q.shape[1]
    t = vt.shape[-1]
    return pl.pallas_call(
        functools.partial(_attn_kernel, t=t),
        grid=(N_HEADS // HEADS_PER_STEP, s // t),
        in_specs=[
            pl.BlockSpec((HEADS_PER_STEP, t, LANES), lambda h, i: (h, i, 0)),
            pl.BlockSpec((HEADS_PER_STEP, s, LANES), lambda h, i: (h, 0, 0)),
            pl.BlockSpec((HEADS_PER_STEP, s // t, VT_ROWS, t), lambda h, i: (h, 0, 0, 0)),
        ],
        out_specs=pl.BlockSpec((HEADS_PER_STEP * V_HEAD, t), lambda h, i: (h, i)),
        out_shape=jax.ShapeDtypeStruct((N_HEADS * V_HEAD, s), BF16),
        scratch_shapes=[
            pltpu.VMEM((t, t), F32),
            pltpu.VMEM((t, t), F32),
            pltpu.VMEM((SUBLANES, t), F32),
            pltpu.VMEM((SUBLANES, t), F32),
            pltpu.VMEM((VT_ROWS, t), F32),
            pltpu.VMEM((SUBLANES, t), F32),
        ],
        compiler_params=_params("parallel", "parallel"),
        name="mla_attention",
    )(q, k, vt)


def _top2_of_4(a0, a1, a2, a3):
    hi1, lo1 = jnp.maximum(a0, a1), jnp.minimum(a0, a1)
    hi2, lo2 = jnp.maximum(a2, a3), jnp.minimum(a2, a3)
    first = jnp.maximum(hi1, hi2)
    second = jnp.maximum(jnp.minimum(hi1, hi2), jnp.maximum(lo1, lo2))
    return first + second


def _argmax_first(vals):
    best, idx = vals[0], jnp.zeros_like(vals[0])
    for i in range(1, len(vals)):
        take = vals[i] > best
        best = jnp.where(take, vals[i], best)
        idx = jnp.where(take, float(i), idx)
    return idx, best


def _pick(idx, rows):
    out = rows[-1]
    for i in range(len(rows) - 2, -1, -1):
        out = jnp.where(idx == float(i), rows[i], out)
    return out


def _store_rows(ref, v):
    n = v.shape[0]
    for c in range(ROW_CHUNKS):
        ref[pl.ds(c, n, stride=ROW_CHUNKS), :] = v[:, c * LANES:(c + 1) * LANES]


def _load_rows(ref):
    n = ref.shape[0] // ROW_CHUNKS
    return jnp.concatenate(
        [ref[pl.ds(c, n, stride=ROW_CHUNKS), :] for c in range(ROW_CHUNKS)], axis=1)


def _route_rows(y, rw_ref, rbias_ref, tri_ref, run_sc):
    y_hi = y.astype(BF16)
    y_lo = (y - y_hi.astype(F32)).astype(BF16)
    prod = _dot(y_hi, rw_ref[...]) + _dot(y_lo, rw_ref[...])
    prod_t = prod.T
    logits = prod_t[:N_EXPERTS] + prod_t[N_EXPERTS:2 * N_EXPERTS]
    scores = 1.0 / (1.0 + jnp.exp(-logits))
    biased = scores + rbias_ref[...]
    b = [biased[e:e + 1] for e in range(N_EXPERTS)]
    sc = [scores[e:e + 1] for e in range(N_EXPERTS)]
    group_scores = [_top2_of_4(*b[g * 4:g * 4 + 4]) for g in range(N_GROUPS)]
    g_idx, _ = _argmax_first(group_scores)
    in_b = [_pick(g_idx, [b[g * 4 + j] for g in range(N_GROUPS)]) for j in range(EXPERTS_PER_GROUP)]
    in_s = [_pick(g_idx, [sc[g * 4 + j] for g in range(N_GROUPS)]) for j in range(EXPERTS_PER_GROUP)]
    l1, _ = _argmax_first(in_b)
    rest = [jnp.where(l1 == float(j), -jnp.inf, in_b[j]) for j in range(EXPERTS_PER_GROUP)]
    l2, _ = _argmax_first(rest)
    w1 = _pick(l1, in_s)
    w2 = _pick(l2, in_s)
    wsum = w1 + w2
    e1 = g_idx * float(EXPERTS_PER_GROUP) + l1
    e2 = g_idx * float(EXPERTS_PER_GROUP) + l2
    eid = lax.broadcasted_iota(jnp.int32, logits.shape, 0).astype(F32)
    hit1 = eid == e1
    hit2 = eid == e2
    member = jnp.where(jnp.logical_or(hit1, hit2), 1.0, 0.0)
    run = run_sc[...]
    before = _dot(member.astype(BF16), tri_ref[...]) + run[:, 0:1]
    r1 = jnp.sum(jnp.where(hit1, before, 0.0), axis=0, keepdims=True)
    r2 = jnp.sum(jnp.where(hit2, before, 0.0), axis=0, keepdims=True)
    run_sc[...] = run + jnp.sum(member, axis=1, keepdims=True)
    zero = jnp.zeros_like(e1)
    return jnp.concatenate([e1, e2, w1 / wsum, w2 / wsum, r1, r2, zero, zero], axis=0)


def _mixer_epilogue(x, h, g_ref, b_ref, rw_ref, rbias_ref, tri_ref, y_out, rt_out, cnt_out, run_sc):
    @pl.when(pl.program_id(0) == 0)
    def _():
        run_sc[...] = jnp.zeros(run_sc.shape, F32)

    y = _layer_norm(DEEPNORM_ALPHA * x + h, g_ref[...], b_ref[...])
    _store_rows(y_out, y)
    rt_out[...] = _route_rows(y, rw_ref, rbias_ref, tri_ref, run_sc)
    cnt_out[...] = run_sc[...]


def _epilogue_specs(ts):
    in_specs = [
        _const_spec((1, D_MODEL)),
        _const_spec((1, D_MODEL)),
        _const_spec((D_MODEL, LANES)),
        _const_spec((N_EXPERTS, 1)),
        _const_spec((ts, ts)),
    ]
    out_specs = [
        pl.BlockSpec((ROW_CHUNKS * ts, LANES), lambda i: (i, 0)),
        pl.BlockSpec((ROUTE_ROWS, ts), lambda i: (0, i)),
        _const_spec((N_EXPERTS, LANES)),
    ]
    return in_specs, out_specs


def _epilogue_shapes(s):
    return [
        jax.ShapeDtypeStruct((ROW_CHUNKS * s, LANES), F32),
        jax.ShapeDtypeStruct((ROUTE_ROWS, s), F32),
        jax.ShapeDtypeStruct((N_EXPERTS, LANES), F32),
    ]


_RUN_SCRATCH = pltpu.VMEM((N_EXPERTS, LANES), F32)


def _attn_out_kernel(x_ref, o_ref, wo_ref, *epilogue_refs):
    h = lax.dot_general(o_ref[...], wo_ref[...], (((0,), (0,)), ((), ())), preferred_element_type=F32)
    _mixer_epilogue(x_ref[...], h, *epilogue_refs)


def _attn_out(x, o, wo, g, b, rw, rbias, tri):
    s = x.shape[0]
    ts = min(TOKEN_TILE, s)
    ep_in, ep_out = _epilogue_specs(ts)
    return pl.pallas_call(
        _attn_out_kernel,
        grid=(s // ts,),
        in_specs=[
            pl.BlockSpec((ts, D_MODEL), lambda i: (i, 0)),
            pl.BlockSpec((N_HEADS * V_HEAD, ts), lambda i: (0, i)),
            _const_spec((N_HEADS * V_HEAD, D_MODEL)),
        ] + ep_in,
        out_specs=ep_out,
        out_shape=_epilogue_shapes(s),
        scratch_shapes=[_RUN_SCRATCH],
        compiler_params=_params("arbitrary"),
        name="attn_out_ln_route",
    )(x, o, wo, g, b, rw, rbias, tri)


def _conv_kernel(x_ref, win_ref, ck_ref, wout_ref, g_ref, b_ref, rw_ref, rbias_ref, tri_ref,
                 y_out, rt_out, cnt_out, u_sc, run_sc, *, ts):
    @pl.when(pl.program_id(0) == 0)
    def _():
        u_sc[0:8, :] = jnp.zeros((8, D_MODEL), F32)

    x = x_ref[...]
    xb = x.astype(BF16)
    c_gate = _dot(xb, win_ref[:, D_MODEL:2 * D_MODEL])
    u = c_gate * _dot(xb, win_ref[:, 2 * D_MODEL:])
    u_sc[8:8 + ts, :] = u
    ck = ck_ref[...]
    conv = u_sc[6:6 + ts, :] * ck[0:1] + u_sc[7:7 + ts, :] * ck[1:2] + u * ck[2:3]
    u_sc[0:8, :] = u_sc[ts:ts + 8, :]
    b_gate = _dot(xb, win_ref[:, :D_MODEL])
    h = _dot((b_gate * conv).astype(BF16), wout_ref[...])
    _mixer_epilogue(x, h, g_ref, b_ref, rw_ref, rbias_ref, tri_ref, y_out, rt_out, cnt_out, run_sc)


def _conv_mixer(x, win, ck, wout, g, b, rw, rbias, tri):
    s = x.shape[0]
    ts = min(TOKEN_TILE, s)
    ep_in, ep_out = _epilogue_specs(ts)
    return pl.pallas_call(
        functools.partial(_conv_kernel, ts=ts),
        grid=(s // ts,),
        in_specs=[
            pl.BlockSpec((ts, D_MODEL), lambda i: (i, 0)),
            _const_spec((D_MODEL, 3 * D_MODEL)),
            _const_spec((CONV_W, D_MODEL)),
            _const_spec((D_MODEL, D_MODEL)),
        ] + ep_in,
        out_specs=ep_out,
        out_shape=_epilogue_shapes(s),
        scratch_shapes=[pltpu.VMEM((ts + 8, D_MODEL), F32), _RUN_SCRATCH],
        compiler_params=_params("arbitrary"),
        name="conv_mixer_ln_route",
    )(x, win, ck, wout, g, b, rw, rbias, tri)


def _tile_plan(counts, tm, n_tiles_max):
    counts = counts[:, 0].astype(jnp.int32)
    tiles_per_expert = (counts + tm - 1) // tm
    tile_end = jnp.cumsum(tiles_per_expert)
    row_start = (tile_end - tiles_per_expert) * tm
    last_tile = jnp.maximum(tile_end - 1, 0)
    n_tiles = tile_end[-1:]
    tile = jnp.minimum(jnp.arange(n_tiles_max, dtype=jnp.int32), n_tiles - 1)
    tile_expert = jnp.sum((tile[:, None] >= tile_end[None, :]).astype(jnp.int32), axis=1)
    return row_start, last_tile, tile_expert, n_tiles


def _row(ref, r):
    return ref.at[pl.ds(pl.multiple_of(r * ROW_CHUNKS, ROW_CHUNKS), ROW_CHUNKS)]


def _dispatch_kernel(last_ref, nt_ref, pos_ref, y_ref, xs_ref, zero_sc, sem, *, ts, tm, n_tiles_max):
    def tile(i):
        return xs_ref.at[pl.ds(pl.multiple_of(i * (ROW_CHUNKS * tm), ROW_CHUNKS * tm), ROW_CHUNKS * tm)]

    @pl.when(pl.program_id(0) == 0)
    def _():
        zero_sc[...] = jnp.zeros(zero_sc.shape, F32)
        for e in range(N_EXPERTS):
            pltpu.make_async_copy(zero_sc, tile(last_ref[e]), sem).start()
        for e in range(N_EXPERTS):
            pltpu.make_async_copy(zero_sc, tile(last_ref[e]), sem).wait()

        def fill(i, carry):
            copy = pltpu.make_async_copy(zero_sc, tile(i), sem)
            copy.start()
            copy.wait()
            return carry

        lax.fori_loop(nt_ref[0], n_tiles_max, fill, 0)

    def issue(t, carry):
        for k in range(2):
            pltpu.make_async_copy(_row(y_ref, t), _row(xs_ref, pos_ref[0, k, t]), sem).start(priority=k)
        return carry

    lax.fori_loop(0, ts, issue, 0, unroll=8)
    for _ in range(2):
        pltpu.make_async_copy(y_ref, xs_ref.at[pl.ds(0, ROW_CHUNKS * ts)], sem).wait()


def _pos_spec(ts):
    return pl.BlockSpec((1, 2, ts), lambda i, *_: (i, 0, 0), memory_space=pltpu.SMEM)


def _dispatch(last_tile, n_tiles, pos, y_rows, n_rows, tm):
    ts = pos.shape[-1]
    s = y_rows.shape[0] // ROW_CHUNKS
    return pl.pallas_call(
        functools.partial(_dispatch_kernel, ts=ts, tm=tm, n_tiles_max=n_rows // tm),
        grid_spec=pltpu.PrefetchScalarGridSpec(
            num_scalar_prefetch=2,
            grid=(s // ts,),
            in_specs=[
                _pos_spec(ts),
                pl.BlockSpec((ROW_CHUNKS * ts, LANES), lambda i, *_: (i, 0)),
            ],
            out_specs=pl.BlockSpec(memory_space=pl.ANY),
            scratch_shapes=[pltpu.VMEM((ROW_CHUNKS * tm, LANES), F32), pltpu.SemaphoreType.DMA],
        ),
        out_shape=jax.ShapeDtypeStruct((ROW_CHUNKS * n_rows, LANES), F32),
        compiler_params=_params("arbitrary"),
        name="moe_dispatch",
    )(last_tile, n_tiles, pos, y_rows)


def _expert_kernel(te_ref, nt_ref, xs_ref, wg_ref, wu_ref, wd_ref, ys_ref, wg_b, wu_b, wd_b):
    i = pl.program_id(0)

    @pl.when(i < nt_ref[0])
    def _():
        @pl.when(jnp.logical_or(i == 0, te_ref[i] != te_ref[jnp.maximum(i - 1, 0)]))
        def _():
            wg_b[...] = wg_ref[0, 0].astype(BF16)
            wu_b[...] = wu_ref[0, 0].astype(BF16)
            wd_b[...] = wd_ref[0, 0].astype(BF16)

        x = _load_rows(xs_ref).astype(BF16)
        hg = _dot(x, wg_b[...])
        hu = _dot(x, wu_b[...])
        hidden = (hg * (1.0 / (1.0 + jnp.exp(-hg)))) * hu
        _store_rows(ys_ref, _dot(hidden.astype(BF16), wd_b[...]))

    @pl.when(i >= nt_ref[0])
    def _():
        ys_ref[...] = jnp.zeros(ys_ref.shape, ys_ref.dtype)


def _experts(tile_expert, n_tiles, xs, w_gate, w_up, w_down, layer, tm):
    n_rows = xs.shape[0] // ROW_CHUNKS
    block = (ROW_CHUNKS * tm, LANES)
    weight = lambda i, te, nt: (layer, te[i], 0, 0)
    return pl.pallas_call(
        _expert_kernel,
        grid_spec=pltpu.PrefetchScalarGridSpec(
            num_scalar_prefetch=2,
            grid=(n_rows // tm,),
            in_specs=[
                pl.BlockSpec(block, lambda i, te, nt: (jnp.minimum(i, nt[0] - 1), 0)),
                pl.BlockSpec((1, 1, D_MODEL, D_EXPERT), weight),
                pl.BlockSpec((1, 1, D_MODEL, D_EXPERT), weight),
                pl.BlockSpec((1, 1, D_EXPERT, D_MODEL), weight),
            ],
            out_specs=pl.BlockSpec(block, lambda i, te, nt: (i, 0)),
            scratch_shapes=[
                pltpu.VMEM((D_MODEL, D_EXPERT), BF16),
                pltpu.VMEM((D_MODEL, D_EXPERT), BF16),
                pltpu.VMEM((D_EXPERT, D_MODEL), BF16),
            ],
        ),
        out_shape=jax.ShapeDtypeStruct(xs.shape, F32),
        compiler_params=_params("arbitrary"),
        name="moe_experts",
    )(tile_expert, n_tiles, xs, w_gate, w_up, w_down)


def _combine_kernel(pos_ref, y_ref, rt_ref, ys_ref, g_ref, b_ref, out_ref, got1, got2, sem, *, ts):
    def issue(t, carry):
        for k, got in enumerate((got1, got2)):
            pltpu.make_async_copy(_row(ys_ref, pos_ref[0, k, t]), _row(got, t), sem).start(priority=k)
        return carry

    lax.fori_loop(0, ts, issue, 0, unroll=8)
    for got in (got1, got2):
        pltpu.make_async_copy(ys_ref.at[pl.ds(0, ROW_CHUNKS * ts)], got, sem).wait()
    rt = rt_ref[...]
    moe = rt[:, 2:3] * _load_rows(got1) + rt[:, 3:4] * _load_rows(got2)
    out_ref[...] = _layer_norm(DEEPNORM_ALPHA * _load_rows(y_ref) + moe, g_ref[...], b_ref[...])


def _combine(pos, y_rows, rt_cols, ys, g, b):
    ts = pos.shape[-1]
    s = y_rows.shape[0] // ROW_CHUNKS
    return pl.pallas_call(
        functools.partial(_combine_kernel, ts=ts),
        grid=(s // ts,),
        in_specs=[
            pl.BlockSpec((1, 2, ts), lambda i: (i, 0, 0), memory_space=pltpu.SMEM),
            pl.BlockSpec((ROW_CHUNKS * ts, LANES), lambda i: (i, 0)),
            pl.BlockSpec((ts, ROUTE_ROWS), lambda i: (i, 0)),
            pl.BlockSpec(memory_space=pl.ANY),
            _const_spec((1, D_MODEL)),
            _const_spec((1, D_MODEL)),
        ],
        out_specs=pl.BlockSpec((ts, D_MODEL), lambda i: (i, 0)),
        out_shape=jax.ShapeDtypeStruct((s, D_MODEL), F32),
        scratch_shapes=[
            pltpu.VMEM((ROW_CHUNKS * ts, LANES), F32),
            pltpu.VMEM((ROW_CHUNKS * ts, LANES), F32),
            pltpu.SemaphoreType.DMA,
        ],
        compiler_params=_params("arbitrary"),
        name="moe_combine_ln",
    )(pos, y_rows, rt_cols, ys, g, b)


def _moe(y_rows, rt, counts, w_gate, w_up, w_down, layer, g, b):
    s = rt.shape[1]
    ts = min(MOVE_TILE, s)
    tm = MOE_TILE
    n_rows = 2 * s + N_EXPERTS * tm
    row_start, last_tile, tile_expert, n_tiles = _tile_plan(counts, tm, n_rows // tm)
    expert = rt[0:2].astype(jnp.int32)
    first_row = sum(jnp.where(expert == e, row_start[e], 0) for e in range(N_EXPERTS))
    pos = (first_row + rt[4:6].astype(jnp.int32)).reshape(2, s // ts, ts).transpose(1, 0, 2)
    xs = _dispatch(last_tile, n_tiles, pos, y_rows, n_rows, tm)
    ys = _experts(tile_expert, n_tiles, xs, w_gate, w_up, w_down, layer, tm)
    return _combine(pos, y_rows, rt.T, ys, g, b)


def _rope_lane_pad(t):
    pad = [(0, 0)] * (t.ndim - 1)
    return jnp.pad(t, pad + [(QK_NOPE, LANES - QK_NOPE - QK_ROPE)])


def _rotate_half(t):
    return jnp.concatenate([-t[..., HALF_ROPE:], t[..., :HALF_ROPE]], axis=-1)


def _prep_mla(w_dqkv, w_uq, w_ukv, w_o):
    w_kr = w_dqkv[:, Q_LORA + KV_LORA:]
    wd = jnp.concatenate(
        [w_dqkv[:, :Q_LORA + KV_LORA], _rope_lane_pad(w_kr), _rope_lane_pad(_rotate_half(w_kr))],
        axis=1).astype(BF16)
    uq = w_uq.reshape(Q_LORA, N_HEADS, QK_NOPE + QK_ROPE)
    zq = jnp.zeros((Q_LORA, N_HEADS, LANES - QK_NOPE - QK_ROPE), F32)
    q_plain = jnp.concatenate([uq, zq], axis=-1)
    q_rot = jnp.concatenate(
        [jnp.zeros((Q_LORA, N_HEADS, QK_NOPE), F32), _rotate_half(uq[..., QK_NOPE:]), zq], axis=-1)
    wq = jnp.concatenate([q_plain, q_rot], axis=-1).reshape(Q_LORA, N_HEADS * 2 * LANES).astype(BF16)
    ukv = w_ukv.reshape(KV_LORA, N_HEADS, QK_NOPE + V_HEAD)
    wk = jnp.pad(ukv[..., :QK_NOPE], ((0, 0), (0, 0), (0, LANES - QK_NOPE)))
    wk = wk.reshape(KV_LORA, N_HEADS * LANES).astype(BF16)
    wvt = jnp.pad(ukv[..., QK_NOPE:], ((0, 0), (0, 0), (0, VT_ROWS - V_HEAD)))
    wvt = wvt.reshape(KV_LORA, N_HEADS * VT_ROWS).T.astype(BF16)
    wo = w_o.astype(BF16)
    return wd, wq, wk, wvt, wo


def _prep_router(router_w, router_bias):
    hi = router_w.astype(BF16)
    lo = (router_w - hi.astype(F32)).astype(BF16)
    rw = jnp.concatenate([hi, lo, jnp.zeros((D_MODEL, LANES - 2 * N_EXPERTS), BF16)], axis=1)
    return rw, router_bias.reshape(N_EXPERTS, 1).astype(F32)


def kernel(x, positions, mla_w_dqkv, mla_q_norm, mla_kv_norm, mla_w_uq, mla_w_ukv, mla_w_o,
           conv_w_in, conv_kernel, conv_w_out, router_w, router_bias,
           moe_w_gate, moe_w_up, moe_w_down, ln_mix_g, ln_mix_b, ln_ffn_g, ln_ffn_b):
    batch, s, _ = x.shape
    rw, rbias = _prep_router(router_w, router_bias)
    inv_freq = ROPE_THETA ** (-jnp.arange(0, QK_ROPE, 2, dtype=F32) / QK_ROPE)
    invf = _rope_lane_pad(jnp.concatenate([inv_freq, inv_freq]))[None, :]
    ts = min(TOKEN_TILE, s)
    tri = jnp.triu(jnp.ones((ts, ts), BF16), k=1)
    row = lambda t: t.reshape(1, -1).astype(F32)

    outs = []
    for bi in range(batch):
        xc = x[bi]
        pos_col = positions[bi].reshape(s, 1)
        for i in range(DEPTH):
            j = i // 2
            if i % 2 == 0:
                wd, wq, wk, wvt, wo = _prep_mla(mla_w_dqkv[j], mla_w_uq[j], mla_w_ukv[j], mla_w_o[j])
                q, k, vt = _mla_proj(xc, pos_col, invf, wd, row(mla_q_norm[j]), row(mla_kv_norm[j]),
                                     wq, wk, wvt)
                o = _attention(q, k, vt)
                y, rt, cnt = _attn_out(xc, o, wo, row(ln_mix_g[i]), row(ln_mix_b[i]), rw, rbias, tri)
            else:
                y, rt, cnt = _conv_mixer(xc, conv_w_in[j].astype(BF16), conv_kernel[j].astype(F32),
                                         conv_w_out[j].astype(BF16), row(ln_mix_g[i]),
                                         row(ln_mix_b[i]), rw, rbias, tri)
            xc = _moe(y, rt, cnt, moe_w_gate, moe_w_up, moe_w_down, i, row(ln_ffn_g[i]), row(ln_ffn_b[i]))
        outs.append(xc)
    return jnp.stack(outs, axis=0)
```

```python
import functools

import jax
import jax.numpy as jnp
from jax import lax
from jax.experimental import pallas as pl
from jax.experimental.pallas import tpu as pltpu

D_MODEL = 1024
DEPTH = 4
N_HEADS = 16
QK_NOPE = 64
QK_ROPE = 32
V_HEAD = 64
Q_LORA = 256
KV_LORA = 128
ROPE_THETA = 10000.0
CONV_W = 3
N_EXPERTS = 16
N_GROUPS = 4
EXPERTS_PER_GROUP = N_EXPERTS // N_GROUPS
D_EXPERT = 512
DEEPNORM_ALPHA = float((2 * DEPTH) ** 0.25)
LN_EPS = 1e-5
RMS_EPS = 1e-6
QK_SCALE = (QK_NOPE + QK_ROPE) ** -0.5 * 1.4426950408889634

LANES = 128
SUBLANES = 8
VT_ROWS = 80
HALF_ROPE = QK_ROPE // 2
LAT_W = Q_LORA + KV_LORA + 2 * LANES
ROUTE_ROWS = 8
ROW_CHUNKS = D_MODEL // LANES
VMEM_LIMIT = 56 * 1024 * 1024

TOKEN_TILE = 512
ATTN_TILE = 512
MOE_TILE = 512
MOVE_TILE = 1024
HEADS_PER_STEP = 2

F32 = jnp.float32
BF16 = jnp.bfloat16
_NT = (((1,), (1,)), ((), ()))


def _params(*sem):
    return pltpu.CompilerParams(dimension_semantics=sem, vmem_limit_bytes=VMEM_LIMIT)


def _const_spec(shape):
    return pl.BlockSpec(shape, lambda *_: (0,) * len(shape))


def _dot(a, b):
    return jnp.dot(a, b, preferred_element_type=F32)


def _layer_norm(z, g, b):
    mu = jnp.mean(z, axis=-1, keepdims=True)
    zc = z - mu
    var = jnp.mean(zc * zc, axis=-1, keepdims=True)
    return zc * lax.rsqrt(var + LN_EPS) * g + b


def _rms_norm(z, g):
    return z * lax.rsqrt(jnp.mean(z * z, axis=-1, keepdims=True) + RMS_EPS) * g


def _mla_proj_kernel(x_ref, pos_ref, invf_ref, wd_ref, qn_ref, kvn_ref, wq_ref, wk_ref, wvt_ref,
                     q_out, k_out, vt_out):
    xb = x_ref[...].astype(BF16)
    lat = _dot(xb, wd_ref[...])
    cq = _rms_norm(lat[:, :Q_LORA], qn_ref[...]).astype(BF16)
    ckv = _rms_norm(lat[:, Q_LORA:Q_LORA + KV_LORA], kvn_ref[...]).astype(BF16)
    kr = lat[:, Q_LORA + KV_LORA:Q_LORA + KV_LORA + LANES]
    krr = lat[:, Q_LORA + KV_LORA + LANES:]
    ang = pos_ref[...].astype(F32) * invf_ref[...]
    cos = jnp.cos(ang)
    sin = jnp.sin(ang)
    k_rope = kr * cos + krr * sin
    ones_row = lax.broadcasted_iota(jnp.int32, (VT_ROWS, x_ref.shape[0]), 0) == V_HEAD
    for h in range(0, N_HEADS, 2):
        kk = _dot(ckv, wk_ref[:, h * LANES:(h + 2) * LANES])
        k_out[h] = (kk[:, :LANES] + k_rope).astype(BF16)
        k_out[h + 1] = (kk[:, LANES:] + k_rope).astype(BF16)
    for h in range(N_HEADS):
        qq = _dot(cq, wq_ref[:, h * 2 * LANES:(h + 1) * 2 * LANES])
        q_out[h] = ((qq[:, :LANES] * cos + qq[:, LANES:] * sin) * QK_SCALE).astype(BF16)
        vt = lax.dot_general(wvt_ref[h * VT_ROWS:(h + 1) * VT_ROWS, :], ckv, _NT,
                             preferred_element_type=F32)
        vt_out[h, 0] = jnp.where(ones_row, 1.0, vt).astype(BF16)


def _mla_proj(x, pos_col, invf, wd, qn, kvn, wq, wk, wvt):
    s = x.shape[0]
    ts = min(ATTN_TILE, s)
    head_spec = pl.BlockSpec((N_HEADS, ts, LANES), lambda i: (0, i, 0))
    out_sds = jax.ShapeDtypeStruct((N_HEADS, s, LANES), BF16)
    return pl.pallas_call(
        _mla_proj_kernel,
        grid=(s // ts,),
        in_specs=[
            pl.BlockSpec((ts, D_MODEL), lambda i: (i, 0)),
            pl.BlockSpec((ts, 1), lambda i: (i, 0)),
            _const_spec((1, LANES)),
            _const_spec((D_MODEL, LAT_W)),
            _const_spec((1, Q_LORA)),
            _const_spec((1, KV_LORA)),
            _const_spec((Q_LORA, N_HEADS * 2 * LANES)),
            _const_spec((KV_LORA, N_HEADS * LANES)),
            _const_spec((N_HEADS * VT_ROWS, KV_LORA)),
        ],
        out_specs=[head_spec, head_spec,
                   pl.BlockSpec((N_HEADS, 1, VT_ROWS, ts), lambda i: (0, i, 0, 0))],
        out_shape=[out_sds, out_sds,
                   jax.ShapeDtypeStruct((N_HEADS, s // ts, VT_ROWS, ts), BF16)],
        compiler_params=_params("parallel"),
        name="mla_proj",
    )(x, pos_col, invf, wd, qn, kvn, wq, wk, wvt)


def _sublane_allmax(part):
    for shift in (4, 2, 1):
        part = jnp.maximum(part, pltpu.roll(part, shift, 0))
    return part


def _attn_kernel(q_ref, k_ref, vt_ref, o_ref, s_a, s_b, mx_a, mx_b, acc_sc, m_sc, *, t):
    qi = pl.program_id(1)
    groups = t // SUBLANES

    def one_head(hh, carry_out):
        q = q_ref[hh]
        m_sc[...] = jnp.full(m_sc.shape, -jnp.inf, F32)
        acc_sc[...] = jnp.zeros(acc_sc.shape, F32)

        def scores(j, s_dst, mx_dst, diagonal):
            k = k_ref[hh, pl.ds(pl.multiple_of(j * t, t), t), :]
            s = lax.dot_general(k, q, _NT, preferred_element_type=F32)
            if diagonal:
                key = lax.broadcasted_iota(jnp.int32, s.shape, 0)
                qry = lax.broadcasted_iota(jnp.int32, s.shape, 1)
                s = jnp.where(key <= qry, s, -jnp.inf)
            s_dst[...] = s
            mx_dst[...] = jnp.max(s.reshape(groups, SUBLANES, t), axis=0)

        def softmax_pv(j, s_src, mx_src):
            m_prev = m_sc[...]
            m_new = jnp.maximum(m_prev, _sublane_allmax(mx_src[...]))
            alpha = jnp.exp2(m_prev - m_new)
            p = jnp.exp2(s_src[...].reshape(groups, SUBLANES, t) - m_new[None])
            pv = _dot(vt_ref[hh, j], p.reshape(t, t).astype(BF16))
            acc = acc_sc[...].reshape(VT_ROWS // SUBLANES, SUBLANES, t) * alpha[None]
            acc_sc[...] = acc.reshape(VT_ROWS, t) + pv
            m_sc[...] = m_new

        scores(0, s_a, mx_a, False)

        @pl.when(qi == 0)
        def _():
            scores(0, s_a, mx_a, True)

        def pair(i, carry):
            j = 2 * i
            scores(j + 1, s_b, mx_b, False)
            softmax_pv(j, s_a, mx_a)
            scores(j + 2, s_a, mx_a, False)
            softmax_pv(j + 1, s_b, mx_b)
            return carry

        def quad(i, carry):
            pair(2 * i, carry)
            return pair(2 * i + 1, carry)

        def octet(i, carry):
            quad(2 * i, carry)
            return quad(2 * i + 1, carry)

        n_pairs = jnp.maximum((qi - 1) // 2, 0)
        n_quads = n_pairs // 2
        n_octets = n_quads // 2
        lax.fori_loop(0, n_octets, octet, 0)
        lax.fori_loop(2 * n_octets, n_quads, quad, 0)
        lax.fori_loop(2 * n_quads, n_pairs, pair, 0)
        j0 = 2 * n_pairs

        @pl.when(jnp.logical_and(qi > 0, qi % 2 == 1))
        def _():
            scores(j0 + 1, s_b, mx_b, True)
            softmax_pv(j0, s_a, mx_a)
            softmax_pv(j0 + 1, s_b, mx_b)

        @pl.when(jnp.logical_and(qi > 0, qi % 2 == 0))
        def _():
            scores(j0 + 1, s_b, mx_b, False)
            softmax_pv(j0, s_a, mx_a)
            scores(j0 + 2, s_a, mx_a, True)
            softmax_pv(j0 + 1, s_b, mx_b)
            softmax_pv(j0 + 2, s_a, mx_a)

        @pl.when(qi == 0)
        def _():
            softmax_pv(0, s_a, mx_a)

        acc = acc_sc[...]
        o = acc[:V_HEAD] / acc[V_HEAD:V_HEAD + 1, :]
        o_ref[pl.ds(pl.multiple_of(hh * V_HEAD, V_HEAD), V_HEAD), :] = o.astype(o_ref.dtype)
        return carry_out

    lax.fori_loop(0, HEADS_PER_STEP, one_head, 0)


def _attention(q, k, vt):
    s = q.shape[1]
    t = vt.shape[-1]
    return pl.pallas_call(
        functools.partial(_attn_kernel, t=t),
        grid=(N_HEADS // HEADS_PER_STEP, s // t),
        in_specs=[
            pl.BlockSpec((HEADS_PER_STEP, t, LANES), lambda h, i: (h, i, 0)),
            pl.BlockSpec((HEADS_PER_STEP, s, LANES), lambda h, i: (h, 0, 0)),
            pl.BlockSpec((HEADS_PER_STEP, s // t, VT_ROWS, t), lambda h, i: (h, 0, 0, 0)),
        ],
        out_specs=pl.BlockSpec((HEADS_PER_STEP * V_HEAD, t), lambda h, i: (h, i)),
        out_shape=jax.ShapeDtypeStruct((N_HEADS * V_HEAD, s), BF16),
        scratch_shapes=[
            pltpu.VMEM((t, t), F32),
            pltpu.VMEM((t, t), F32),
            pltpu.VMEM((SUBLANES, t), F32),
            pltpu.VMEM((SUBLANES, t), F32),
            pltpu.VMEM((VT_ROWS, t), F32),
            pltpu.VMEM((SUBLANES, t), F32),
        ],
        compiler_params=_params("parallel", "parallel"),
        name="mla_attention",
    )(q, k, vt)


def _top2_of_4(a0, a1, a2, a3):
    hi1, lo1 = jnp.maximum(a0, a1), jnp.minimum(a0, a1)
    hi2, lo2 = jnp.maximum(a2, a3), jnp.minimum(a2, a3)
    first = jnp.maximum(hi1, hi2)
    second = jnp.maximum(jnp.minimum(hi1, hi2), jnp.maximum(lo1, lo2))
    return first + second


def _argmax_first(vals):
    best, idx = vals[0], jnp.zeros_like(vals[0])
    for i in range(1, len(vals)):
        take = vals[i] > best
        best = jnp.where(take, vals[i], best)
        idx = jnp.where(take, float(i), idx)
    return idx, best


def _pick(idx, rows):
    out = rows[-1]
    for i in range(len(rows) - 2, -1, -1):
        out = jnp.where(idx == float(i), rows[i], out)
    return out


def _store_rows(ref, v):
    n = v.shape[0]
    for c in range(ROW_CHUNKS):
        ref[pl.ds(c, n, stride=ROW_CHUNKS), :] = v[:, c * LANES:(c + 1) * LANES]


def _load_rows(ref):
    n = ref.shape[0] // ROW_CHUNKS
    return jnp.concatenate(
        [ref[pl.ds(c, n, stride=ROW_CHUNKS), :] for c in range(ROW_CHUNKS)], axis=1)


def _route_rows(y, rw_ref, rbias_ref, tri_ref, run_sc):
    y_hi = y.astype(BF16)
    y_lo = (y - y_hi.astype(F32)).astype(BF16)
    prod = _dot(y_hi, rw_ref[...]) + _dot(y_lo, rw_ref[...])
    prod_t = prod.T
    logits = prod_t[:N_EXPERTS] + prod_t[N_EXPERTS:2 * N_EXPERTS]
    scores = 1.0 / (1.0 + jnp.exp(-logits))
    biased = scores + rbias_ref[...]
    b = [biased[e:e + 1] for e in range(N_EXPERTS)]
    sc = [scores[e:e + 1] for e in range(N_EXPERTS)]
    group_scores = [_top2_of_4(*b[g * 4:g * 4 + 4]) for g in range(N_GROUPS)]
    g_idx, _ = _argmax_first(group_scores)
    in_b = [_pick(g_idx, [b[g * 4 + j] for g in range(N_GROUPS)]) for j in range(EXPERTS_PER_GROUP)]
    in_s = [_pick(g_idx, [sc[g * 4 + j] for g in range(N_GROUPS)]) for j in range(EXPERTS_PER_GROUP)]
    l1, _ = _argmax_first(in_b)
    rest = [jnp.where(l1 == float(j), -jnp.inf, in_b[j]) for j in range(EXPERTS_PER_GROUP)]
    l2, _ = _argmax_first(rest)
    w1 = _pick(l1, in_s)
    w2 = _pick(l2, in_s)
    wsum = w1 + w2
    e1 = g_idx * float(EXPERTS_PER_GROUP) + l1
    e2 = g_idx * float(EXPERTS_PER_GROUP) + l2
    eid = lax.broadcasted_iota(jnp.int32, logits.shape, 0).astype(F32)
    hit1 = eid == e1
    hit2 = eid == e2
    member = jnp.where(jnp.logical_or(hit1, hit2), 1.0, 0.0)
    run = run_sc[...]
    before = _dot(member.astype(BF16), tri_ref[...]) + run[:, 0:1]
    r1 = jnp.sum(jnp.where(hit1, before, 0.0), axis=0, keepdims=True)
    r2 = jnp.sum(jnp.where(hit2, before, 0.0), axis=0, keepdims=True)
    run_sc[...] = run + jnp.sum(member, axis=1, keepdims=True)
    zero = jnp.zeros_like(e1)
    return jnp.concatenate([e1, e2, w1 / wsum, w2 / wsum, r1, r2, zero, zero], axis=0)


def _mixer_epilogue(x, h, g_ref, b_ref, rw_ref, rbias_ref, tri_ref, y_out, rt_out, cnt_out, run_sc):
    @pl.when(pl.program_id(0) == 0)
    def _():
        run_sc[...] = jnp.zeros(run_sc.shape, F32)

    y = _layer_norm(DEEPNORM_ALPHA * x + h, g_ref[...], b_ref[...])
    _store_rows(y_out, y)
    rt_out[...] = _route_rows(y, rw_ref, rbias_ref, tri_ref, run_sc)
    cnt_out[...] = run_sc[...]


def _epilogue_specs(ts):
    in_specs = [
        _const_spec((1, D_MODEL)),
        _const_spec((1, D_MODEL)),
        _const_spec((D_MODEL, LANES)),
        _const_spec((N_EXPERTS, 1)),
        _const_spec((ts, ts)),
    ]
    out_specs = [
        pl.BlockSpec((ROW_CHUNKS * ts, LANES), lambda i: (i, 0)),
        pl.BlockSpec((ROUTE_ROWS, ts), lambda i: (0, i)),
        _const_spec((N_EXPERTS, LANES)),
    ]
    return in_specs, out_specs


def _epilogue_shapes(s):
    return [
        jax.ShapeDtypeStruct((ROW_CHUNKS * s, LANES), F32),
        jax.ShapeDtypeStruct((ROUTE_ROWS, s), F32),
        jax.ShapeDtypeStruct((N_EXPERTS, LANES), F32),
    ]


_RUN_SCRATCH = pltpu.VMEM((N_EXPERTS, LANES), F32)


def _attn_out_kernel(x_ref, o_ref, wo_ref, *epilogue_refs):
    h = lax.dot_general(o_ref[...], wo_ref[...], (((0,), (0,)), ((), ())), preferred_element_type=F32)
    _mixer_epilogue(x_ref[...], h, *epilogue_refs)


def _attn_out(x, o, wo, g, b, rw, rbias, tri):
    s = x.shape[0]
    ts = min(TOKEN_TILE, s)
    ep_in, ep_out = _epilogue_specs(ts)
    return pl.pallas_call(
        _attn_out_kernel,
        grid=(s // ts,),
        in_specs=[
            pl.BlockSpec((ts, D_MODEL), lambda i: (i, 0)),
            pl.BlockSpec((N_HEADS * V_HEAD, ts), lambda i: (0, i)),
            _const_spec((N_HEADS * V_HEAD, D_MODEL)),
        ] + ep_in,
        out_specs=ep_out,
        out_shape=_epilogue_shapes(s),
        scratch_shapes=[_RUN_SCRATCH],
        compiler_params=_params("arbitrary"),
        name="attn_out_ln_route",
    )(x, o, wo, g, b, rw, rbias, tri)


def _conv_kernel(x_ref, win_ref, ck_ref, wout_ref, g_ref, b_ref, rw_ref, rbias_ref, tri_ref,
                 y_out, rt_out, cnt_out, u_sc, run_sc, *, ts):
    @pl.when(pl.program_id(0) == 0)
    def _():
        u_sc[0:8, :] = jnp.zeros((8, D_MODEL), F32)

    x = x_ref[...]
    xb = x.astype(BF16)
    c_gate = _dot(xb, win_ref[:, D_MODEL:2 * D_MODEL])
    u = c_gate * _dot(xb, win_ref[:, 2 * D_MODEL:])
    u_sc[8:8 + ts, :] = u
    ck = ck_ref[...]
    conv = u_sc[6:6 + ts, :] * ck[0:1] + u_sc[7:7 + ts, :] * ck[1:2] + u * ck[2:3]
    u_sc[0:8, :] = u_sc[ts:ts + 8, :]
    b_gate = _dot(xb, win_ref[:, :D_MODEL])
    h = _dot((b_gate * conv).astype(BF16), wout_ref[...])
    _mixer_epilogue(x, h, g_ref, b_ref, rw_ref, rbias_ref, tri_ref, y_out, rt_out, cnt_out, run_sc)


def _conv_mixer(x, win, ck, wout, g, b, rw, rbias, tri):
    s = x.shape[0]
    ts = min(TOKEN_TILE, s)
    ep_in, ep_out = _epilogue_specs(ts)
    return pl.pallas_call(
        functools.partial(_conv_kernel, ts=ts),
        grid=(s // ts,),
        in_specs=[
            pl.BlockSpec((ts, D_MODEL), lambda i: (i, 0)),
            _const_spec((D_MODEL, 3 * D_MODEL)),
            _const_spec((CONV_W, D_MODEL)),
            _const_spec((D_MODEL, D_MODEL)),
        ] + ep_in,
        out_specs=ep_out,
        out_shape=_epilogue_shapes(s),
        scratch_shapes=[pltpu.VMEM((ts + 8, D_MODEL), F32), _RUN_SCRATCH],
        compiler_params=_params("arbitrary"),
        name="conv_mixer_ln_route",
    )(x, win, ck, wout, g, b, rw, rbias, tri)


def _tile_plan(counts, tm, n_tiles_max):
    counts = counts[:, 0].astype(jnp.int32)
    tiles_per_expert = (counts + tm - 1) // tm
    tile_end = jnp.cumsum(tiles_per_expert)
    row_start = (tile_end - tiles_per_expert) * tm
    last_tile = jnp.maximum(tile_end - 1, 0)
    n_tiles = tile_end[-1:]
    tile = jnp.minimum(jnp.arange(n_tiles_max, dtype=jnp.int32), n_tiles - 1)
    tile_expert = jnp.sum((tile[:, None] >= tile_end[None, :]).astype(jnp.int32), axis=1)
    return row_start, last_tile, tile_expert, n_tiles


def _row(ref, r):
    return ref.at[pl.ds(pl.multiple_of(r * ROW_CHUNKS, ROW_CHUNKS), ROW_CHUNKS)]


def _dispatch_kernel(last_ref, nt_ref, pos_ref, y_ref, xs_ref, zero_sc, sem, *, ts, tm, n_tiles_max):
    def tile(i):
        return xs_ref.at[pl.ds(pl.multiple_of(i * (ROW_CHUNKS * tm), ROW_CHUNKS * tm), ROW_CHUNKS * tm)]

    @pl.when(pl.program_id(0) == 0)
    def _():
        zero_sc[...] = jnp.zeros(zero_sc.shape, F32)
        for e in range(N_EXPERTS):
            pltpu.make_async_copy(zero_sc, tile(last_ref[e]), sem).start()
        for e in range(N_EXPERTS):
            pltpu.make_async_copy(zero_sc, tile(last_ref[e]), sem).wait()

        def fill(i, carry):
            copy = pltpu.make_async_copy(zero_sc, tile(i), sem)
            copy.start()
            copy.wait()
            return carry

        lax.fori_loop(nt_ref[0], n_tiles_max, fill, 0)

    def issue(t, carry):
        for k in range(2):
            pltpu.make_async_copy(_row(y_ref, t), _row(xs_ref, pos_ref[0, k, t]), sem).start(priority=k)
        return carry

    lax.fori_loop(0, ts, issue, 0, unroll=8)
    for _ in range(2):
        pltpu.make_async_copy(y_ref, xs_ref.at[pl.ds(0, ROW_CHUNKS * ts)], sem).wait()


def _pos_spec(ts):
    return pl.BlockSpec((1, 2, ts), lambda i, *_: (i, 0, 0), memory_space=pltpu.SMEM)


def _dispatch(last_tile, n_tiles, pos, y_rows, n_rows, tm):
    ts = pos.shape[-1]
    s = y_rows.shape[0] // ROW_CHUNKS
    return pl.pallas_call(
        functools.partial(_dispatch_kernel, ts=ts, tm=tm, n_tiles_max=n_rows // tm),
        grid_spec=pltpu.PrefetchScalarGridSpec(
            num_scalar_prefetch=2,
            grid=(s // ts,),
            in_specs=[
                _pos_spec(ts),
                pl.BlockSpec((ROW_CHUNKS * ts, LANES), lambda i, *_: (i, 0)),
            ],
            out_specs=pl.BlockSpec(memory_space=pl.ANY),
            scratch_shapes=[pltpu.VMEM((ROW_CHUNKS * tm, LANES), F32), pltpu.SemaphoreType.DMA],
        ),
        out_shape=jax.ShapeDtypeStruct((ROW_CHUNKS * n_rows, LANES), F32),
        compiler_params=_params("arbitrary"),
        name="moe_dispatch",
    )(last_tile, n_tiles, pos, y_rows)


def _expert_kernel(te_ref, nt_ref, xs_ref, wg_ref, wu_ref, wd_ref, ys_ref, wg_b, wu_b, wd_b):
    i = pl.program_id(0)

    @pl.when(i < nt_ref[0])
    def _():
        @pl.when(jnp.logical_or(i == 0, te_ref[i] != te_ref[jnp.maximum(i - 1, 0)]))
        def _():
            wg_b[...] = wg_ref[0, 0].astype(BF16)
            wu_b[...] = wu_ref[0, 0].astype(BF16)
            wd_b[...] = wd_ref[0, 0].astype(BF16)

        x = _load_rows(xs_ref).astype(BF16)
        hg = _dot(x, wg_b[...])
        hu = _dot(x, wu_b[...])
        hidden = (hg * (1.0 / (1.0 + jnp.exp(-hg)))) * hu
        _store_rows(ys_ref, _dot(hidden.astype(BF16), wd_b[...]))

    @pl.when(i >= nt_ref[0])
    def _():
        ys_ref[...] = jnp.zeros(ys_ref.shape, ys_ref.dtype)


def _experts(tile_expert, n_tiles, xs, w_gate, w_up, w_down, layer, tm):
    n_rows = xs.shape[0] // ROW_CHUNKS
    block = (ROW_CHUNKS * tm, LANES)
    weight = lambda i, te, nt: (layer, te[i], 0, 0)
    return pl.pallas_call(
        _expert_kernel,
        grid_spec=pltpu.PrefetchScalarGridSpec(
            num_scalar_prefetch=2,
            grid=(n_rows // tm,),
            in_specs=[
                pl.BlockSpec(block, lambda i, te, nt: (jnp.minimum(i, nt[0] - 1), 0)),
                pl.BlockSpec((1, 1, D_MODEL, D_EXPERT), weight),
                pl.BlockSpec((1, 1, D_MODEL, D_EXPERT), weight),
                pl.BlockSpec((1, 1, D_EXPERT, D_MODEL), weight),
            ],
            out_specs=pl.BlockSpec(block, lambda i, te, nt: (i, 0)),
            scratch_shapes=[
                pltpu.VMEM((D_MODEL, D_EXPERT), BF16),
                pltpu.VMEM((D_MODEL, D_EXPERT), BF16),
                pltpu.VMEM((D_EXPERT, D_MODEL), BF16),
            ],
        ),
        out_shape=jax.ShapeDtypeStruct(xs.shape, F32),
        compiler_params=_params("arbitrary"),
        name="moe_experts",
    )(tile_expert, n_tiles, xs, w_gate, w_up, w_down)


def _combine_kernel(pos_ref, y_ref, rt_ref, ys_ref, g_ref, b_ref, out_ref, got1, got2, sem, *, ts):
    def issue(t, carry):
        for k, got in enumerate((got1, got2)):
            pltpu.make_async_copy(_row(ys_ref, pos_ref[0, k, t]), _row(got, t), sem).start(priority=k)
        return carry

    lax.fori_loop(0, ts, issue, 0, unroll=8)
    for got in (got1, got2):
        pltpu.make_async_copy(ys_ref.at[pl.ds(0, ROW_CHUNKS * ts)], got, sem).wait()
    rt = rt_ref[...]
    moe = rt[:, 2:3] * _load_rows(got1) + rt[:, 3:4] * _load_rows(got2)
    out_ref[...] = _layer_norm(DEEPNORM_ALPHA * _load_rows(y_ref) + moe, g_ref[...], b_ref[...])


def _combine(pos, y_rows, rt_cols, ys, g, b):
    ts = pos.shape[-1]
    s = y_rows.shape[0] // ROW_CHUNKS
    return pl.pallas_call(
        functools.partial(_combine_kernel, ts=ts),
        grid=(s // ts,),
        in_specs=[
            pl.BlockSpec((1, 2, ts), lambda i: (i, 0, 0), memory_space=pltpu.SMEM),
            pl.BlockSpec((ROW_CHUNKS * ts, LANES), lambda i: (i, 0)),
            pl.BlockSpec((ts, ROUTE_ROWS), lambda i: (i, 0)),
            pl.BlockSpec(memory_space=pl.ANY),
            _const_spec((1, D_MODEL)),
            _const_spec((1, D_MODEL)),
        ],
        out_specs=pl.BlockSpec((ts, D_MODEL), lambda i: (i, 0)),
        out_shape=jax.ShapeDtypeStruct((s, D_MODEL), F32),
        scratch_shapes=[
            pltpu.VMEM((ROW_CHUNKS * ts, LANES), F32),
            pltpu.VMEM((ROW_CHUNKS * ts, LANES), F32),
            pltpu.SemaphoreType.DMA,
        ],
        compiler_params=_params("arbitrary"),
        name="moe_combine_ln",
    )(pos, y_rows, rt_cols, ys, g, b)


def _moe(y_rows, rt, counts, w_gate, w_up, w_down, layer, g, b):
    s = rt.shape[1]
    ts = min(MOVE_TILE, s)
    tm = MOE_TILE
    n_rows = 2 * s + N_EXPERTS * tm
    row_start, last_tile, tile_expert, n_tiles = _tile_plan(counts, tm, n_rows // tm)
    expert = rt[0:2].reshape(-1, LANES)
    first_row = sum(jnp.where(expert == float(e), row_start[e], 0) for e in range(N_EXPERTS))
    pos = first_row + rt[4:6].reshape(-1, LANES).astype(jnp.int32)
    pos = pos.reshape(2, s // ts, ts).transpose(1, 0, 2)
    xs = _dispatch(last_tile, n_tiles, pos, y_rows, n_rows, tm)
    ys = _experts(tile_expert, n_tiles, xs, w_gate, w_up, w_down, layer, tm)
    return _combine(pos, y_rows, rt.T, ys, g, b)


def _rope_lane_pad(t):
    pad = [(0, 0)] * (t.ndim - 1)
    return jnp.pad(t, pad + [(QK_NOPE, LANES - QK_NOPE - QK_ROPE)])


def _rotate_half(t):
    return jnp.concatenate([-t[..., HALF_ROPE:], t[..., :HALF_ROPE]], axis=-1)


def _prep_mla(w_dqkv, w_uq, w_ukv, w_o):
    w_kr = w_dqkv[:, Q_LORA + KV_LORA:]
    wd = jnp.concatenate(
        [w_dqkv[:, :Q_LORA + KV_LORA], _rope_lane_pad(w_kr), _rope_lane_pad(_rotate_half(w_kr))],
        axis=1).astype(BF16)
    uq = w_uq.reshape(Q_LORA, N_HEADS, QK_NOPE + QK_ROPE)
    zq = jnp.zeros((Q_LORA, N_HEADS, LANES - QK_NOPE - QK_ROPE), F32)
    q_plain = jnp.concatenate([uq, zq], axis=-1)
    q_rot = jnp.concatenate(
        [jnp.zeros((Q_LORA, N_HEADS, QK_NOPE), F32), _rotate_half(uq[..., QK_NOPE:]), zq], axis=-1)
    wq = jnp.concatenate([q_plain, q_rot], axis=-1).reshape(Q_LORA, N_HEADS * 2 * LANES).astype(BF16)
    ukv = w_ukv.reshape(KV_LORA, N_HEADS, QK_NOPE + V_HEAD)
    wk = jnp.pad(ukv[..., :QK_NOPE], ((0, 0), (0, 0), (0, LANES - QK_NOPE)))
    wk = wk.reshape(KV_LORA, N_HEADS * LANES).astype(BF16)
    wvt = jnp.pad(ukv[..., QK_NOPE:], ((0, 0), (0, 0), (0, VT_ROWS - V_HEAD)))
    wvt = wvt.reshape(KV_LORA, N_HEADS * VT_ROWS).T.astype(BF16)
    wo = w_o.astype(BF16)
    return wd, wq, wk, wvt, wo


def _prep_router(router_w, router_bias):
    hi = router_w.astype(BF16)
    lo = (router_w - hi.astype(F32)).astype(BF16)
    rw = jnp.concatenate([hi, lo, jnp.zeros((D_MODEL, LANES - 2 * N_EXPERTS), BF16)], axis=1)
    return rw, router_bias.reshape(N_EXPERTS, 1).astype(F32)


def kernel(x, positions, mla_w_dqkv, mla_q_norm, mla_kv_norm, mla_w_uq, mla_w_ukv, mla_w_o,
           conv_w_in, conv_kernel, conv_w_out, router_w, router_bias,
           moe_w_gate, moe_w_up, moe_w_down, ln_mix_g, ln_mix_b, ln_ffn_g, ln_ffn_b):
    batch, s, _ = x.shape
    rw, rbias = _prep_router(router_w, router_bias)
    inv_freq = ROPE_THETA ** (-jnp.arange(0, QK_ROPE, 2, dtype=F32) / QK_ROPE)
    invf = _rope_lane_pad(jnp.concatenate([inv_freq, inv_freq]))[None, :]
    ts = min(TOKEN_TILE, s)
    tri = jnp.triu(jnp.ones((ts, ts), BF16), k=1)
    row = lambda t: t.reshape(1, -1).astype(F32)

    outs = []
    for bi in range(batch):
        xc = x[bi]
        pos_col = positions[bi].reshape(s, 1)
        for i in range(DEPTH):
            j = i // 2
            if i % 2 == 0:
                wd, wq, wk, wvt, wo = _prep_mla(mla_w_dqkv[j], mla_w_uq[j], mla_w_ukv[j], mla_w_o[j])
                q, k, vt = _mla_proj(xc, pos_col, invf, wd, row(mla_q_norm[j]), row(mla_kv_norm[j]),
                                     wq, wk, wvt)
                o = _attention(q, k, vt)
                y, rt, cnt = _attn_out(xc, o, wo, row(ln_mix_g[i]), row(ln_mix_b[i]), rw, rbias, tri)
            else:
                y, rt, cnt = _conv_mixer(xc, conv_w_in[j].astype(BF16), conv_kernel[j].astype(F32),
                                         conv_w_out[j].astype(BF16), row(ln_mix_g[i]),
                                         row(ln_mix_b[i]), rw, rbias, tri)
            xc = _moe(y, rt, cnt, moe_w_gate, moe_w_up, moe_w_down, i, row(ln_ffn_g[i]), row(ln_ffn_b[i]))
        outs.append(xc)
    return jnp.stack(outs, axis=0)
```

```python
import functools

import jax
import jax.numpy as jnp
from jax import lax
from jax.experimental import pallas as pl
from jax.experimental.pallas import tpu as pltpu

D_MODEL = 1024
DEPTH = 4
N_HEADS = 16
QK_NOPE = 64
QK_ROPE = 32
V_HEAD = 64
Q_LORA = 256
KV_LORA = 128
ROPE_THETA = 10000.0
CONV_W = 3
N_EXPERTS = 16
N_GROUPS = 4
EXPERTS_PER_GROUP = N_EXPERTS // N_GROUPS
D_EXPERT = 512
DEEPNORM_ALPHA = float((2 * DEPTH) ** 0.25)
LN_EPS = 1e-5
RMS_EPS = 1e-6
QK_SCALE = (QK_NOPE + QK_ROPE) ** -0.5 * 1.4426950408889634

LANES = 128
SUBLANES = 8
VT_ROWS = 80
HALF_ROPE = QK_ROPE // 2
LAT_W = Q_LORA + KV_LORA + 2 * LANES
ROUTE_ROWS = 8
ROW_CHUNKS = D_MODEL // LANES
VMEM_LIMIT = 56 * 1024 * 1024

TOKEN_TILE = 512
ATTN_TILE = 512
MOE_TILE = 512
MOVE_TILE = 1024
HEADS_PER_STEP = 2

F32 = jnp.float32
BF16 = jnp.bfloat16
_NT = (((1,), (1,)), ((), ()))


def _params(*sem):
    return pltpu.CompilerParams(dimension_semantics=sem, vmem_limit_bytes=VMEM_LIMIT)


def _const_spec(shape):
    return pl.BlockSpec(shape, lambda *_: (0,) * len(shape))


def _dot(a, b):
    return jnp.dot(a, b, preferred_element_type=F32)


def _layer_norm(z, g, b):
    mu = jnp.mean(z, axis=-1, keepdims=True)
    zc = z - mu
    var = jnp.mean(zc * zc, axis=-1, keepdims=True)
    return zc * lax.rsqrt(var + LN_EPS) * g + b


def _rms_norm(z, g):
    return z * lax.rsqrt(jnp.mean(z * z, axis=-1, keepdims=True) + RMS_EPS) * g


def _mla_proj_kernel(x_ref, pos_ref, invf_ref, wd_ref, qn_ref, kvn_ref, wq_ref, wk_ref, wvt_ref,
                     q_out, k_out, vt_out):
    xb = x_ref[...].astype(BF16)
    lat = _dot(xb, wd_ref[...])
    cq = _rms_norm(lat[:, :Q_LORA], qn_ref[...]).astype(BF16)
    ckv = _rms_norm(lat[:, Q_LORA:Q_LORA + KV_LORA], kvn_ref[...]).astype(BF16)
    kr = lat[:, Q_LORA + KV_LORA:Q_LORA + KV_LORA + LANES]
    krr = lat[:, Q_LORA + KV_LORA + LANES:]
    ang = pos_ref[...].astype(F32) * invf_ref[...]
    cos = jnp.cos(ang)
    sin = jnp.sin(ang)
    k_rope = kr * cos + krr * sin
    ones_row = lax.broadcasted_iota(jnp.int32, (VT_ROWS, x_ref.shape[0]), 0) == V_HEAD
    for h in range(N_HEADS):
        qq = _dot(cq, wq_ref[:, h * 2 * LANES:(h + 1) * 2 * LANES])
        q_out[h] = ((qq[:, :LANES] * cos + qq[:, LANES:] * sin) * QK_SCALE).astype(BF16)
        k_out[h] = (_dot(ckv, wk_ref[:, h * LANES:(h + 1) * LANES]) + k_rope).astype(BF16)
        vt = lax.dot_general(wvt_ref[h * VT_ROWS:(h + 1) * VT_ROWS, :], ckv, _NT,
                             preferred_element_type=F32)
        vt_out[h, 0] = jnp.where(ones_row, 1.0, vt).astype(BF16)


def _mla_proj(x, pos_col, invf, wd, qn, kvn, wq, wk, wvt):
    s = x.shape[0]
    ts = min(ATTN_TILE, s)
    head_spec = pl.BlockSpec((N_HEADS, ts, LANES), lambda i: (0, i, 0))
    out_sds = jax.ShapeDtypeStruct((N_HEADS, s, LANES), BF16)
    return pl.pallas_call(
        _mla_proj_kernel,
        grid=(s // ts,),
        in_specs=[
            pl.BlockSpec((ts, D_MODEL), lambda i: (i, 0)),
            pl.BlockSpec((ts, 1), lambda i: (i, 0)),
            _const_spec((1, LANES)),
            _const_spec((D_MODEL, LAT_W)),
            _const_spec((1, Q_LORA)),
            _const_spec((1, KV_LORA)),
            _const_spec((Q_LORA, N_HEADS * 2 * LANES)),
            _const_spec((KV_LORA, N_HEADS * LANES)),
            _const_spec((N_HEADS * VT_ROWS, KV_LORA)),
        ],
        out_specs=[head_spec, head_spec,
                   pl.BlockSpec((N_HEADS, 1, VT_ROWS, ts), lambda i: (0, i, 0, 0))],
        out_shape=[out_sds, out_sds,
                   jax.ShapeDtypeStruct((N_HEADS, s // ts, VT_ROWS, ts), BF16)],
        compiler_params=_params("parallel"),
        name="mla_proj",
    )(x, pos_col, invf, wd, qn, kvn, wq, wk, wvt)


def _sublane_allmax(part):
    for shift in (4, 2, 1):
        part = jnp.maximum(part, pltpu.roll(part, shift, 0))
    return part


def _attn_kernel(q_ref, k_ref, vt_ref, o_ref, s_a, s_b, mx_a, mx_b, acc_sc, m_sc, *, t):
    qi = pl.program_id(1)
    groups = t // SUBLANES

    def one_head(hh, carry_out):
        q = q_ref[hh]
        m_sc[...] = jnp.full(m_sc.shape, -jnp.inf, F32)
        acc_sc[...] = jnp.zeros(acc_sc.shape, F32)

        def scores(j, s_dst, mx_dst, diagonal):
            k = k_ref[hh, pl.ds(pl.multiple_of(j * t, t), t), :]
            s = lax.dot_general(k, q, _NT, preferred_element_type=F32)
            if diagonal:
                key = lax.broadcasted_iota(jnp.int32, s.shape, 0)
                qry = lax.broadcasted_iota(jnp.int32, s.shape, 1)
                s = jnp.where(key <= qry, s, -jnp.inf)
            s_dst[...] = s
            mx_dst[...] = jnp.max(s.reshape(groups, SUBLANES, t), axis=0)

        def softmax_pv(j, s_src, mx_src):
            m_prev = m_sc[...]
            m_new = jnp.maximum(m_prev, _sublane_allmax(mx_src[...]))
            alpha = jnp.exp2(m_prev - m_new)
            p = jnp.exp2(s_src[...].reshape(groups, SUBLANES, t) - m_new[None])
            pv = _dot(vt_ref[hh, j], p.reshape(t, t).astype(BF16))
            acc = acc_sc[...].reshape(VT_ROWS // SUBLANES, SUBLANES, t) * alpha[None]
            acc_sc[...] = acc.reshape(VT_ROWS, t) + pv
            m_sc[...] = m_new

        scores(0, s_a, mx_a, False)

        @pl.when(qi == 0)
        def _():
            scores(0, s_a, mx_a, True)

        def pair(i, carry):
            j = 2 * i
            scores(j + 1, s_b, mx_b, False)
            softmax_pv(j, s_a, mx_a)
            scores(j + 2, s_a, mx_a, False)
            softmax_pv(j + 1, s_b, mx_b)
            return carry

        def quad(i, carry):
            pair(2 * i, carry)
            return pair(2 * i + 1, carry)

        def octet(i, carry):
            quad(2 * i, carry)
            return quad(2 * i + 1, carry)

        n_pairs = jnp.maximum((qi - 1) // 2, 0)
        n_quads = n_pairs // 2
        n_octets = n_quads // 2
        lax.fori_loop(0, n_octets, octet, 0)
        lax.fori_loop(2 * n_octets, n_quads, quad, 0)
        lax.fori_loop(2 * n_quads, n_pairs, pair, 0)
        j0 = 2 * n_pairs

        @pl.when(jnp.logical_and(qi > 0, qi % 2 == 1))
        def _():
            scores(j0 + 1, s_b, mx_b, True)
            softmax_pv(j0, s_a, mx_a)
            softmax_pv(j0 + 1, s_b, mx_b)

        @pl.when(jnp.logical_and(qi > 0, qi % 2 == 0))
        def _():
            scores(j0 + 1, s_b, mx_b, False)
            softmax_pv(j0, s_a, mx_a)
            scores(j0 + 2, s_a, mx_a, True)
            softmax_pv(j0 + 1, s_b, mx_b)
            softmax_pv(j0 + 2, s_a, mx_a)

        @pl.when(qi == 0)
        def _():
            softmax_pv(0, s_a, mx_a)

        acc = acc_sc[...]
        o = acc[:V_HEAD] / acc[V_HEAD:V_HEAD + 1, :]
        o_ref[pl.ds(pl.multiple_of(hh * V_HEAD, V_HEAD), V_HEAD), :] = o.astype(o_ref.dtype)
        return carry_out

    lax.fori_loop(0, HEADS_PER_STEP, one_head, 0)


def _attention(q, k, vt):
    s = q.shape[1]
    t = vt.shape[-1]
    return pl.pallas_call(
        functools.partial(_attn_kernel, t=t),
        grid=(N_HEADS // HEADS_PER_STEP, s // t),
        in_specs=[
            pl.BlockSpec((HEADS_PER_STEP, t, LANES), lambda h, i: (h, i, 0)),
            pl.BlockSpec((HEADS_PER_STEP, s, LANES), lambda h, i: (h, 0, 0)),
            pl.BlockSpec((HEADS_PER_STEP, s // t, VT_ROWS, t), lambda h, i: (h, 0, 0, 0)),
        ],
        out_specs=pl.BlockSpec((HEADS_PER_STEP * V_HEAD, t), lambda h, i: (h, i)),
        out_shape=jax.ShapeDtypeStruct((N_HEADS * V_HEAD, s), BF16),
        scratch_shapes=[
            pltpu.VMEM((t, t), F32),
            pltpu.VMEM((t, t), F32),
            pltpu.VMEM((SUBLANES, t), F32),
            pltpu.VMEM((SUBLANES, t), F32),
            pltpu.VMEM((VT_ROWS, t), F32),
            pltpu.VMEM((SUBLANES, t), F32),
        ],
        compiler_params=_params("parallel", "parallel"),
        name="mla_attention",
    )(q, k, vt)


def _top2_of_4(a0, a1, a2, a3):
    hi1, lo1 = jnp.maximum(a0, a1), jnp.minimum(a0, a1)
    hi2, lo2 = jnp.maximum(a2, a3), jnp.minimum(a2, a3)
    first = jnp.maximum(hi1, hi2)
    second = jnp.maximum(jnp.minimum(hi1, hi2), jnp.maximum(lo1, lo2))
    return first + second


def _argmax_first(vals):
    best, idx = vals[0], jnp.zeros_like(vals[0])
    for i in range(1, len(vals)):
        take = vals[i] > best
        best = jnp.where(take, vals[i], best)
        idx = jnp.where(take, float(i), idx)
    return idx, best


def _pick(idx, rows):
    out = rows[-1]
    for i in range(len(rows) - 2, -1, -1):
        out = jnp.where(idx == float(i), rows[i], out)
    return out


def _store_rows(ref, v):
    n = v.shape[0]
    for c in range(ROW_CHUNKS):
        ref[pl.ds(c, n, stride=ROW_CHUNKS), :] = v[:, c * LANES:(c + 1) * LANES]


def _load_rows(ref):
    n = ref.shape[0] // ROW_CHUNKS
    return jnp.concatenate(
        [ref[pl.ds(c, n, stride=ROW_CHUNKS), :] for c in range(ROW_CHUNKS)], axis=1)


def _route_rows(y, rw_ref, rbias_ref, tri_ref, run_sc):
    y_hi = y.astype(BF16)
    y_lo = (y - y_hi.astype(F32)).astype(BF16)
    prod = _dot(y_hi, rw_ref[...]) + _dot(y_lo, rw_ref[...])
    prod_t = prod.T
    logits = prod_t[:N_EXPERTS] + prod_t[N_EXPERTS:2 * N_EXPERTS]
    scores = 1.0 / (1.0 + jnp.exp(-logits))
    biased = scores + rbias_ref[...]
    b = [biased[e:e + 1] for e in range(N_EXPERTS)]
    sc = [scores[e:e + 1] for e in range(N_EXPERTS)]
    group_scores = [_top2_of_4(*b[g * 4:g * 4 + 4]) for g in range(N_GROUPS)]
    g_idx, _ = _argmax_first(group_scores)
    in_b = [_pick(g_idx, [b[g * 4 + j] for g in range(N_GROUPS)]) for j in range(EXPERTS_PER_GROUP)]
    in_s = [_pick(g_idx, [sc[g * 4 + j] for g in range(N_GROUPS)]) for j in range(EXPERTS_PER_GROUP)]
    l1, _ = _argmax_first(in_b)
    rest = [jnp.where(l1 == float(j), -jnp.inf, in_b[j]) for j in range(EXPERTS_PER_GROUP)]
    l2, _ = _argmax_first(rest)
    w1 = _pick(l1, in_s)
    w2 = _pick(l2, in_s)
    wsum = w1 + w2
    e1 = g_idx * float(EXPERTS_PER_GROUP) + l1
    e2 = g_idx * float(EXPERTS_PER_GROUP) + l2
    eid = lax.broadcasted_iota(jnp.int32, logits.shape, 0).astype(F32)
    hit1 = eid == e1
    hit2 = eid == e2
    member = jnp.where(jnp.logical_or(hit1, hit2), 1.0, 0.0)
    run = run_sc[...]
    before = _dot(member.astype(BF16), tri_ref[...]) + run[:, 0:1]
    r1 = jnp.sum(jnp.where(hit1, before, 0.0), axis=0, keepdims=True)
    r2 = jnp.sum(jnp.where(hit2, before, 0.0), axis=0, keepdims=True)
    run_sc[...] = run + jnp.sum(member, axis=1, keepdims=True)
    zero = jnp.zeros_like(e1)
    return jnp.concatenate([e1, e2, w1 / wsum, w2 / wsum, r1, r2, zero, zero], axis=0)


def _mixer_epilogue(x, h, g_ref, b_ref, rw_ref, rbias_ref, tri_ref, y_out, rt_out, cnt_out, run_sc):
    @pl.when(pl.program_id(0) == 0)
    def _():
        run_sc[...] = jnp.zeros(run_sc.shape, F32)

    y = _layer_norm(DEEPNORM_ALPHA * x + h, g_ref[...], b_ref[...])
    _store_rows(y_out, y)
    rt_out[...] = _route_rows(y, rw_ref, rbias_ref, tri_ref, run_sc)
    cnt_out[...] = run_sc[...]


def _epilogue_specs(ts):
    in_specs = [
        _const_spec((1, D_MODEL)),
        _const_spec((1, D_MODEL)),
        _const_spec((D_MODEL, LANES)),
        _const_spec((N_EXPERTS, 1)),
        _const_spec((ts, ts)),
    ]
    out_specs = [
        pl.BlockSpec((ROW_CHUNKS * ts, LANES), lambda i: (i, 0)),
        pl.BlockSpec((ROUTE_ROWS, ts), lambda i: (0, i)),
        _const_spec((N_EXPERTS, LANES)),
    ]
    return in_specs, out_specs


def _epilogue_shapes(s):
    return [
        jax.ShapeDtypeStruct((ROW_CHUNKS * s, LANES), F32),
        jax.ShapeDtypeStruct((ROUTE_ROWS, s), F32),
        jax.ShapeDtypeStruct((N_EXPERTS, LANES), F32),
    ]


_RUN_SCRATCH = pltpu.VMEM((N_EXPERTS, LANES), F32)


def _attn_out_kernel(x_ref, o_ref, wo_ref, *epilogue_refs):
    h = lax.dot_general(o_ref[...], wo_ref[...], (((0,), (0,)), ((), ())), preferred_element_type=F32)
    _mixer_epilogue(x_ref[...], h, *epilogue_refs)


def _attn_out(x, o, wo, g, b, rw, rbias, tri):
    s = x.shape[0]
    ts = min(TOKEN_TILE, s)
    ep_in, ep_out = _epilogue_specs(ts)
    return pl.pallas_call(
        _attn_out_kernel,
        grid=(s // ts,),
        in_specs=[
            pl.BlockSpec((ts, D_MODEL), lambda i: (i, 0)),
            pl.BlockSpec((N_HEADS * V_HEAD, ts), lambda i: (0, i)),
            _const_spec((N_HEADS * V_HEAD, D_MODEL)),
        ] + ep_in,
        out_specs=ep_out,
        out_shape=_epilogue_shapes(s),
        scratch_shapes=[_RUN_SCRATCH],
        compiler_params=_params("arbitrary"),
        name="attn_out_ln_route",
    )(x, o, wo, g, b, rw, rbias, tri)


def _conv_kernel(x_ref, win_ref, ck_ref, wout_ref, g_ref, b_ref, rw_ref, rbias_ref, tri_ref,
                 y_out, rt_out, cnt_out, u_sc, run_sc, *, ts):
    @pl.when(pl.program_id(0) == 0)
    def _():
        u_sc[0:8, :] = jnp.zeros((8, D_MODEL), F32)

    x = x_ref[...]
    xb = x.astype(BF16)
    c_gate = _dot(xb, win_ref[:, D_MODEL:2 * D_MODEL])
    u = c_gate * _dot(xb, win_ref[:, 2 * D_MODEL:])
    u_sc[8:8 + ts, :] = u
    ck = ck_ref[...]
    conv = u_sc[6:6 + ts, :] * ck[0:1] + u_sc[7:7 + ts, :] * ck[1:2] + u * ck[2:3]
    u_sc[0:8, :] = u_sc[ts:ts + 8, :]
    b_gate = _dot(xb, win_ref[:, :D_MODEL])
    h = _dot((b_gate * conv).astype(BF16), wout_ref[...])
    _mixer_epilogue(x, h, g_ref, b_ref, rw_ref, rbias_ref, tri_ref, y_out, rt_out, cnt_out, run_sc)


def _conv_mixer(x, win, ck, wout, g, b, rw, rbias, tri):
    s = x.shape[0]
    ts = min(TOKEN_TILE, s)
    ep_in, ep_out = _epilogue_specs(ts)
    return pl.pallas_call(
        functools.partial(_conv_kernel, ts=ts),
        grid=(s // ts,),
        in_specs=[
            pl.BlockSpec((ts, D_MODEL), lambda i: (i, 0)),
            _const_spec((D_MODEL, 3 * D_MODEL)),
            _const_spec((CONV_W, D_MODEL)),
            _const_spec((D_MODEL, D_MODEL)),
        ] + ep_in,
        out_specs=ep_out,
        out_shape=_epilogue_shapes(s),
        scratch_shapes=[pltpu.VMEM((ts + 8, D_MODEL), F32), _RUN_SCRATCH],
        compiler_params=_params("arbitrary"),
        name="conv_mixer_ln_route",
    )(x, win, ck, wout, g, b, rw, rbias, tri)


def _tile_plan(counts, tm, n_tiles_max):
    counts = counts[:, 0].astype(jnp.int32)
    tiles_per_expert = (counts + tm - 1) // tm
    tile_end = jnp.cumsum(tiles_per_expert)
    row_start = (tile_end - tiles_per_expert) * tm
    last_tile = jnp.maximum(tile_end - 1, 0)
    n_tiles = tile_end[-1:]
    tile = jnp.minimum(jnp.arange(n_tiles_max, dtype=jnp.int32), n_tiles - 1)
    tile_expert = jnp.sum((tile[:, None] >= tile_end[None, :]).astype(jnp.int32), axis=1)
    return row_start, last_tile, tile_expert, n_tiles


def _row(ref, r):
    return ref.at[pl.ds(pl.multiple_of(r * ROW_CHUNKS, ROW_CHUNKS), ROW_CHUNKS)]


def _dispatch_kernel(last_ref, nt_ref, pos_ref, y_ref, xs_ref, zero_sc, stage, sem, row_sems,
                     *, ts, tm, n_tiles_max):
    step = pl.program_id(0)
    slot = step % 2

    def tile(i):
        return xs_ref.at[pl.ds(pl.multiple_of(i * (ROW_CHUNKS * tm), ROW_CHUNKS * tm), ROW_CHUNKS * tm)]

    @pl.when(pl.program_id(0) == 0)
    def _():
        zero_sc[...] = jnp.zeros(zero_sc.shape, F32)
        for e in range(N_EXPERTS):
            pltpu.make_async_copy(zero_sc, tile(last_ref[e]), sem).start()
        for e in range(N_EXPERTS):
            pltpu.make_async_copy(zero_sc, tile(last_ref[e]), sem).wait()

        def fill(i, carry):
            copy = pltpu.make_async_copy(zero_sc, tile(i), sem)
            copy.start()
            copy.wait()
            return carry

        lax.fori_loop(nt_ref[0], n_tiles_max, fill, 0)

    stage[slot] = y_ref[...]

    def issue(t, carry):
        for k in range(2):
            pltpu.make_async_copy(_row(stage.at[slot], t), _row(xs_ref, pos_ref[0, k, t]),
                                  row_sems.at[slot]).start(priority=k)
        return carry

    lax.fori_loop(0, ts, issue, 0, unroll=8)

    def drain(which):
        for _ in range(2):
            pltpu.make_async_copy(stage.at[which], xs_ref.at[pl.ds(0, ROW_CHUNKS * ts)],
                                  row_sems.at[which]).wait()

    @pl.when(step > 0)
    def _():
        drain(1 - slot)

    @pl.when(step == pl.num_programs(0) - 1)
    def _():
        drain(slot)


def _pos_spec(ts):
    return pl.BlockSpec((1, 2, ts), lambda i, *_: (i, 0, 0), memory_space=pltpu.SMEM)


def _dispatch(last_tile, n_tiles, pos, y_rows, n_rows, tm):
    ts = pos.shape[-1]
    s = y_rows.shape[0] // ROW_CHUNKS
    return pl.pallas_call(
        functools.partial(_dispatch_kernel, ts=ts, tm=tm, n_tiles_max=n_rows // tm),
        grid_spec=pltpu.PrefetchScalarGridSpec(
            num_scalar_prefetch=2,
            grid=(s // ts,),
            in_specs=[
                _pos_spec(ts),
                pl.BlockSpec((ROW_CHUNKS * ts, LANES), lambda i, *_: (i, 0)),
            ],
            out_specs=pl.BlockSpec(memory_space=pl.ANY),
            scratch_shapes=[
                pltpu.VMEM((ROW_CHUNKS * tm, LANES), F32),
                pltpu.VMEM((2, ROW_CHUNKS * ts, LANES), F32),
                pltpu.SemaphoreType.DMA,
                pltpu.SemaphoreType.DMA((2,)),
            ],
        ),
        out_shape=jax.ShapeDtypeStruct((ROW_CHUNKS * n_rows, LANES), F32),
        compiler_params=_params("arbitrary"),
        name="moe_dispatch",
    )(last_tile, n_tiles, pos, y_rows)


def _expert_kernel(te_ref, nt_ref, xs_ref, wg_ref, wu_ref, wd_ref, ys_ref, wg_b, wu_b, wd_b):
    i = pl.program_id(0)

    @pl.when(i < nt_ref[0])
    def _():
        @pl.when(jnp.logical_or(i == 0, te_ref[i] != te_ref[jnp.maximum(i - 1, 0)]))
        def _():
            wg_b[...] = wg_ref[0, 0].astype(BF16)
            wu_b[...] = wu_ref[0, 0].astype(BF16)
            wd_b[...] = wd_ref[0, 0].astype(BF16)

        x = _load_rows(xs_ref).astype(BF16)
        hg = _dot(x, wg_b[...])
        hu = _dot(x, wu_b[...])
        hidden = (hg * (1.0 / (1.0 + jnp.exp(-hg)))) * hu
        _store_rows(ys_ref, _dot(hidden.astype(BF16), wd_b[...]))

    @pl.when(i >= nt_ref[0])
    def _():
        ys_ref[...] = jnp.zeros(ys_ref.shape, ys_ref.dtype)


def _experts(tile_expert, n_tiles, xs, w_gate, w_up, w_down, layer, tm):
    n_rows = xs.shape[0] // ROW_CHUNKS
    block = (ROW_CHUNKS * tm, LANES)
    weight = lambda i, te, nt: (layer, te[i], 0, 0)
    return pl.pallas_call(
        _expert_kernel,
        grid_spec=pltpu.PrefetchScalarGridSpec(
            num_scalar_prefetch=2,
            grid=(n_rows // tm,),
            in_specs=[
                pl.BlockSpec(block, lambda i, te, nt: (jnp.minimum(i, nt[0] - 1), 0)),
                pl.BlockSpec((1, 1, D_MODEL, D_EXPERT), weight),
                pl.BlockSpec((1, 1, D_MODEL, D_EXPERT), weight),
                pl.BlockSpec((1, 1, D_EXPERT, D_MODEL), weight),
            ],
            out_specs=pl.BlockSpec(block, lambda i, te, nt: (i, 0)),
            scratch_shapes=[
                pltpu.VMEM((D_MODEL, D_EXPERT), BF16),
                pltpu.VMEM((D_MODEL, D_EXPERT), BF16),
                pltpu.VMEM((D_EXPERT, D_MODEL), BF16),
            ],
        ),
        out_shape=jax.ShapeDtypeStruct(xs.shape, F32),
        compiler_params=_params("arbitrary"),
        name="moe_experts",
    )(tile_expert, n_tiles, xs, w_gate, w_up, w_down)


def _combine_kernel(pos_ref, pos_next_ref, y_ref, rt_ref, ys_ref, g_ref, b_ref, out_ref, got1, got2, sems,
                    *, ts):
    i = pl.program_id(0)
    slot = i % 2

    def gather(tile_pos_ref, into):
        def issue(t, carry):
            for k, got in enumerate((got1, got2)):
                pltpu.make_async_copy(_row(ys_ref, tile_pos_ref[0, k, t]), _row(got.at[into], t),
                                      sems.at[into]).start(priority=k)
            return carry

        lax.fori_loop(0, ts, issue, 0, unroll=8)

    @pl.when(i == 0)
    def _():
        gather(pos_ref, slot)

    @pl.when(i + 1 < pl.num_programs(0))
    def _():
        gather(pos_next_ref, 1 - slot)

    for got in (got1, got2):
        pltpu.make_async_copy(ys_ref.at[pl.ds(0, ROW_CHUNKS * ts)], got.at[slot], sems.at[slot]).wait()
    rt = rt_ref[...]
    moe = rt[:, 2:3] * _load_rows(got1.at[slot]) + rt[:, 3:4] * _load_rows(got2.at[slot])
    out_ref[...] = _layer_norm(DEEPNORM_ALPHA * _load_rows(y_ref) + moe, g_ref[...], b_ref[...])


def _combine(pos, y_rows, rt_cols, ys, g, b):
    ts = pos.shape[-1]
    s = y_rows.shape[0] // ROW_CHUNKS
    return pl.pallas_call(
        functools.partial(_combine_kernel, ts=ts),
        grid=(s // ts,),
        in_specs=[
            pl.BlockSpec((1, 2, ts), lambda i: (i, 0, 0), memory_space=pltpu.SMEM),
            pl.BlockSpec((1, 2, ts), lambda i: (jnp.minimum(i + 1, s // ts - 1), 0, 0),
                         memory_space=pltpu.SMEM),
            pl.BlockSpec((ROW_CHUNKS * ts, LANES), lambda i: (i, 0)),
            pl.BlockSpec((ts, ROUTE_ROWS), lambda i: (i, 0)),
            pl.BlockSpec(memory_space=pl.ANY),
            _const_spec((1, D_MODEL)),
            _const_spec((1, D_MODEL)),
        ],
        out_specs=pl.BlockSpec((ts, D_MODEL), lambda i: (i, 0)),
        out_shape=jax.ShapeDtypeStruct((s, D_MODEL), F32),
        scratch_shapes=[
            pltpu.VMEM((2, ROW_CHUNKS * ts, LANES), F32),
            pltpu.VMEM((2, ROW_CHUNKS * ts, LANES), F32),
            pltpu.SemaphoreType.DMA((2,)),
        ],
        compiler_params=_params("arbitrary"),
        name="moe_combine_ln",
    )(pos, pos, y_rows, rt_cols, ys, g, b)


def _moe(y_rows, rt, counts, w_gate, w_up, w_down, layer, g, b):
    s = rt.shape[1]
    ts = min(MOVE_TILE, s)
    tm = MOE_TILE
    n_rows = 2 * s + N_EXPERTS * tm
    row_start, last_tile, tile_expert, n_tiles = _tile_plan(counts, tm, n_rows // tm)
    expert = rt[0:2].astype(jnp.int32)
    first_row = sum(jnp.where(expert == e, row_start[e], 0) for e in range(N_EXPERTS))
    pos = (first_row + rt[4:6].astype(jnp.int32)).reshape(2, s // ts, ts).transpose(1, 0, 2)
    xs = _dispatch(last_tile, n_tiles, pos, y_rows, n_rows, tm)
    ys = _experts(tile_expert, n_tiles, xs, w_gate, w_up, w_down, layer, tm)
    return _combine(pos, y_rows, rt.T, ys, g, b)


def _rope_lane_pad(t):
    pad = [(0, 0)] * (t.ndim - 1)
    return jnp.pad(t, pad + [(QK_NOPE, LANES - QK_NOPE - QK_ROPE)])


def _rotate_half(t):
    return jnp.concatenate([-t[..., HALF_ROPE:], t[..., :HALF_ROPE]], axis=-1)


def _prep_mla(w_dqkv, w_uq, w_ukv, w_o):
    w_kr = w_dqkv[:, Q_LORA + KV_LORA:]
    wd = jnp.concatenate(
        [w_dqkv[:, :Q_LORA + KV_LORA], _rope_lane_pad(w_kr), _rope_lane_pad(_rotate_half(w_kr))],
        axis=1).astype(BF16)
    uq = w_uq.reshape(Q_LORA, N_HEADS, QK_NOPE + QK_ROPE)
    zq = jnp.zeros((Q_LORA, N_HEADS, LANES - QK_NOPE - QK_ROPE), F32)
    q_plain = jnp.concatenate([uq, zq], axis=-1)
    q_rot = jnp.concatenate(
        [jnp.zeros((Q_LORA, N_HEADS, QK_NOPE), F32), _rotate_half(uq[..., QK_NOPE:]), zq], axis=-1)
    wq = jnp.concatenate([q_plain, q_rot], axis=-1).reshape(Q_LORA, N_HEADS * 2 * LANES).astype(BF16)
    ukv = w_ukv.reshape(KV_LORA, N_HEADS, QK_NOPE + V_HEAD)
    wk = jnp.pad(ukv[..., :QK_NOPE], ((0, 0), (0, 0), (0, LANES - QK_NOPE)))
    wk = wk.reshape(KV_LORA, N_HEADS * LANES).astype(BF16)
    wvt = jnp.pad(ukv[..., QK_NOPE:], ((0, 0), (0, 0), (0, VT_ROWS - V_HEAD)))
    wvt = wvt.reshape(KV_LORA, N_HEADS * VT_ROWS).T.astype(BF16)
    wo = w_o.astype(BF16)
    return wd, wq, wk, wvt, wo


def _prep_router(router_w, router_bias):
    hi = router_w.astype(BF16)
    lo = (router_w - hi.astype(F32)).astype(BF16)
    rw = jnp.concatenate([hi, lo, jnp.zeros((D_MODEL, LANES - 2 * N_EXPERTS), BF16)], axis=1)
    return rw, router_bias.reshape(N_EXPERTS, 1).astype(F32)


def kernel(x, positions, mla_w_dqkv, mla_q_norm, mla_kv_norm, mla_w_uq, mla_w_ukv, mla_w_o,
           conv_w_in, conv_kernel, conv_w_out, router_w, router_bias,
           moe_w_gate, moe_w_up, moe_w_down, ln_mix_g, ln_mix_b, ln_ffn_g, ln_ffn_b):
    batch, s, _ = x.shape
    rw, rbias = _prep_router(router_w, router_bias)
    inv_freq = ROPE_THETA ** (-jnp.arange(0, QK_ROPE, 2, dtype=F32) / QK_ROPE)
    invf = _rope_lane_pad(jnp.concatenate([inv_freq, inv_freq]))[None, :]
    ts = min(TOKEN_TILE, s)
    tri = jnp.triu(jnp.ones((ts, ts), BF16), k=1)
    row = lambda t: t.reshape(1, -1).astype(F32)

    outs = []
    for bi in range(batch):
        xc = x[bi]
        pos_col = positions[bi].reshape(s, 1)
        for i in range(DEPTH):
            j = i // 2
            if i % 2 == 0:
                wd, wq, wk, wvt, wo = _prep_mla(mla_w_dqkv[j], mla_w_uq[j], mla_w_ukv[j], mla_w_o[j])
                q, k, vt = _mla_proj(xc, pos_col, invf, wd, row(mla_q_norm[j]), row(mla_kv_norm[j]),
                                     wq, wk, wvt)
                o = _attention(q, k, vt)
                y, rt, cnt = _attn_out(xc, o, wo, row(ln_mix_g[i]), row(ln_mix_b[i]), rw, rbias, tri)
            else:
                y, rt, cnt = _conv_mixer(xc, conv_w_in[j].astype(BF16), conv_kernel[j].astype(F32),
                                         conv_w_out[j].astype(BF16), row(ln_mix_g[i]),
                                         row(ln_mix_b[i]), rw, rbias, tri)
            xc = _moe(y, rt, cnt, moe_w_gate, moe_w_up, moe_w_down, i, row(ln_ffn_g[i]), row(ln_ffn_b[i]))
        outs.append(xc)
    return jnp.stack(outs, axis=0)
```

```python
import functools

import jax
import jax.numpy as jnp
from jax import lax
from jax.experimental import pallas as pl
from jax.experimental.pallas import tpu as pltpu

D_MODEL = 1024
DEPTH = 4
N_HEADS = 16
QK_NOPE = 64
QK_ROPE = 32
V_HEAD = 64
Q_LORA = 256
KV_LORA = 128
ROPE_THETA = 10000.0
CONV_W = 3
N_EXPERTS = 16
N_GROUPS = 4
EXPERTS_PER_GROUP = N_EXPERTS // N_GROUPS
D_EXPERT = 512
DEEPNORM_ALPHA = float((2 * DEPTH) ** 0.25)
LN_EPS = 1e-5
RMS_EPS = 1e-6
QK_SCALE = (QK_NOPE + QK_ROPE) ** -0.5 * 1.4426950408889634

LANES = 128
SUBLANES = 8
VT_ROWS = 80
HALF_ROPE = QK_ROPE // 2
LAT_W = Q_LORA + KV_LORA + 2 * LANES
ROUTE_ROWS = 8
ROW_CHUNKS = D_MODEL // LANES
VMEM_LIMIT = 56 * 1024 * 1024

TOKEN_TILE = 512
ATTN_TILE = 512
MOE_TILE = 512
MOVE_TILE = 1024
HEADS_PER_STEP = 2

F32 = jnp.float32
BF16 = jnp.bfloat16
_NT = (((1,), (1,)), ((), ()))


def _params(*sem):
    return pltpu.CompilerParams(dimension_semantics=sem, vmem_limit_bytes=VMEM_LIMIT)


def _const_spec(shape):
    return pl.BlockSpec(shape, lambda *_: (0,) * len(shape))


def _dot(a, b):
    return jnp.dot(a, b, preferred_element_type=F32)


def _layer_norm(z, g, b):
    mu = jnp.mean(z, axis=-1, keepdims=True)
    zc = z - mu
    var = jnp.mean(zc * zc, axis=-1, keepdims=True)
    return zc * lax.rsqrt(var + LN_EPS) * g + b


def _rms_norm(z, g):
    return z * lax.rsqrt(jnp.mean(z * z, axis=-1, keepdims=True) + RMS_EPS) * g


def _mla_proj_kernel(x_ref, pos_ref, invf_ref, wd_ref, qn_ref, kvn_ref, wq_ref, wk_ref, wvt_ref,
                     q_out, k_out, vt_out):
    xb = x_ref[...].astype(BF16)
    lat = _dot(xb, wd_ref[...])
    cq = _rms_norm(lat[:, :Q_LORA], qn_ref[...]).astype(BF16)
    ckv = _rms_norm(lat[:, Q_LORA:Q_LORA + KV_LORA], kvn_ref[...]).astype(BF16)
    kr = lat[:, Q_LORA + KV_LORA:Q_LORA + KV_LORA + LANES]
    krr = lat[:, Q_LORA + KV_LORA + LANES:]
    ang = pos_ref[...].astype(F32) * invf_ref[...]
    cos = jnp.cos(ang)
    sin = jnp.sin(ang)
    k_rope = kr * cos + krr * sin
    ones_row = lax.broadcasted_iota(jnp.int32, (VT_ROWS, x_ref.shape[0]), 0) == V_HEAD
    for h in range(N_HEADS):
        qq = _dot(cq, wq_ref[:, h * 2 * LANES:(h + 1) * 2 * LANES])
        q_out[h] = ((qq[:, :LANES] * cos + qq[:, LANES:] * sin) * QK_SCALE).astype(BF16)
        k_out[h] = (_dot(ckv, wk_ref[:, h * LANES:(h + 1) * LANES]) + k_rope).astype(BF16)
        vt = lax.dot_general(wvt_ref[h * VT_ROWS:(h + 1) * VT_ROWS, :], ckv, _NT,
                             preferred_element_type=F32)
        vt_out[h, 0] = jnp.where(ones_row, 1.0, vt).astype(BF16)


def _mla_proj(x, pos_col, invf, wd, qn, kvn, wq, wk, wvt):
    s = x.shape[0]
    ts = min(ATTN_TILE, s)
    head_spec = pl.BlockSpec((N_HEADS, ts, LANES), lambda i: (0, i, 0))
    out_sds = jax.ShapeDtypeStruct((N_HEADS, s, LANES), BF16)
    return pl.pallas_call(
        _mla_proj_kernel,
        grid=(s // ts,),
        in_specs=[
            pl.BlockSpec((ts, D_MODEL), lambda i: (i, 0)),
            pl.BlockSpec((ts, 1), lambda i: (i, 0)),
            _const_spec((1, LANES)),
            _const_spec((D_MODEL, LAT_W)),
            _const_spec((1, Q_LORA)),
            _const_spec((1, KV_LORA)),
            _const_spec((Q_LORA, N_HEADS * 2 * LANES)),
            _const_spec((KV_LORA, N_HEADS * LANES)),
            _const_spec((N_HEADS * VT_ROWS, KV_LORA)),
        ],
        out_specs=[head_spec, head_spec,
                   pl.BlockSpec((N_HEADS, 1, VT_ROWS, ts), lambda i: (0, i, 0, 0))],
        out_shape=[out_sds, out_sds,
                   jax.ShapeDtypeStruct((N_HEADS, s // ts, VT_ROWS, ts), BF16)],
        compiler_params=_params("parallel"),
        name="mla_proj",
    )(x, pos_col, invf, wd, qn, kvn, wq, wk, wvt)


def _sublane_allmax(part):
    for shift in (4, 2, 1):
        part = jnp.maximum(part, pltpu.roll(part, shift, 0))
    return part


def _attn_kernel(q_ref, k_ref, vt_ref, o_ref, s_a, s_b, mx_a, mx_b, acc_sc, m_sc, *, t):
    qi = pl.program_id(1)
    groups = t // SUBLANES

    def one_head(hh, carry_out):
        q = q_ref[hh]
        m_sc[...] = jnp.full(m_sc.shape, -jnp.inf, F32)
        acc_sc[...] = jnp.zeros(acc_sc.shape, F32)

        def scores(j, s_dst, mx_dst, diagonal):
            k = k_ref[hh, pl.ds(pl.multiple_of(j * t, t), t), :]
            s = lax.dot_general(k, q, _NT, preferred_element_type=F32)
            if diagonal:
                key = lax.broadcasted_iota(jnp.int32, s.shape, 0)
                qry = lax.broadcasted_iota(jnp.int32, s.shape, 1)
                s = jnp.where(key <= qry, s, -jnp.inf)
            s_dst[...] = s
            mx_dst[...] = jnp.max(s.reshape(groups, SUBLANES, t), axis=0)

        def softmax_pv(j, s_src, mx_src):
            m_prev = m_sc[...]
            m_new = jnp.maximum(m_prev, _sublane_allmax(mx_src[...]))
            alpha = jnp.exp2(m_prev - m_new)
            p = jnp.exp2(s_src[...].reshape(groups, SUBLANES, t) - m_new[None])
            pv = _dot(vt_ref[hh, j], p.reshape(t, t).astype(BF16))
            acc = acc_sc[...].reshape(VT_ROWS // SUBLANES, SUBLANES, t) * alpha[None]
            acc_sc[...] = acc.reshape(VT_ROWS, t) + pv
            m_sc[...] = m_new

        scores(0, s_a, mx_a, False)

        @pl.when(qi == 0)
        def _():
            scores(0, s_a, mx_a, True)

        def pair(i, carry):
            j = 2 * i
            scores(j + 1, s_b, mx_b, False)
            softmax_pv(j, s_a, mx_a)
            scores(j + 2, s_a, mx_a, False)
            softmax_pv(j + 1, s_b, mx_b)
            return carry

        def quad(i, carry):
            pair(2 * i, carry)
            return pair(2 * i + 1, carry)

        def octet(i, carry):
            quad(2 * i, carry)
            return quad(2 * i + 1, carry)

        def sixteen(i, carry):
            octet(2 * i, carry)
            return octet(2 * i + 1, carry)

        n_pairs = jnp.maximum((qi - 1) // 2, 0)
        n_quads = n_pairs // 2
        n_octets = n_quads // 2
        n_sixteens = n_octets // 2
        lax.fori_loop(0, n_sixteens, sixteen, 0)
        lax.fori_loop(2 * n_sixteens, n_octets, octet, 0)
        lax.fori_loop(2 * n_octets, n_quads, quad, 0)
        lax.fori_loop(2 * n_quads, n_pairs, pair, 0)
        j0 = 2 * n_pairs

        @pl.when(jnp.logical_and(qi > 0, qi % 2 == 1))
        def _():
            scores(j0 + 1, s_b, mx_b, True)
            softmax_pv(j0, s_a, mx_a)
            softmax_pv(j0 + 1, s_b, mx_b)

        @pl.when(jnp.logical_and(qi > 0, qi % 2 == 0))
        def _():
            scores(j0 + 1, s_b, mx_b, False)
            softmax_pv(j0, s_a, mx_a)
            scores(j0 + 2, s_a, mx_a, True)
            softmax_pv(j0 + 1, s_b, mx_b)
            softmax_pv(j0 + 2, s_a, mx_a)

        @pl.when(qi == 0)
        def _():
            softmax_pv(0, s_a, mx_a)

        acc = acc_sc[...]
        o = acc[:V_HEAD] / acc[V_HEAD:V_HEAD + 1, :]
        o_ref[pl.ds(pl.multiple_of(hh * V_HEAD, V_HEAD), V_HEAD), :] = o.astype(o_ref.dtype)
        return carry_out

    lax.fori_loop(0, HEADS_PER_STEP, one_head, 0)


def _attention(q, k, vt):
    s = q.shape[1]
    t = vt.shape[-1]
    return pl.pallas_call(
        functools.partial(_attn_kernel, t=t),
        grid=(N_HEADS // HEADS_PER_STEP, s // t),
        in_specs=[
            pl.BlockSpec((HEADS_PER_STEP, t, LANES), lambda h, i: (h, i, 0)),
            pl.BlockSpec((HEADS_PER_STEP, s, LANES), lambda h, i: (h, 0, 0)),
            pl.BlockSpec((HEADS_PER_STEP, s // t, VT_ROWS, t), lambda h, i: (h, 0, 0, 0)),
        ],
        out_specs=pl.BlockSpec((HEADS_PER_STEP * V_HEAD, t), lambda h, i: (h, i)),
        out_shape=jax.ShapeDtypeStruct((N_HEADS * V_HEAD, s), BF16),
        scratch_shapes=[
            pltpu.VMEM((t, t), F32),
            pltpu.VMEM((t, t), F32),
            pltpu.VMEM((SUBLANES, t), F32),
            pltpu.VMEM((SUBLANES, t), F32),
            pltpu.VMEM((VT_ROWS, t), F32),
            pltpu.VMEM((SUBLANES, t), F32),
        ],
        compiler_params=_params("parallel", "parallel"),
        name="mla_attention",
    )(q, k, vt)


def _top2_of_4(a0, a1, a2, a3):
    hi1, lo1 = jnp.maximum(a0, a1), jnp.minimum(a0, a1)
    hi2, lo2 = jnp.maximum(a2, a3), jnp.minimum(a2, a3)
    first = jnp.maximum(hi1, hi2)
    second = jnp.maximum(jnp.minimum(hi1, hi2), jnp.maximum(lo1, lo2))
    return first + second


def _argmax_first(vals):
    best, idx = vals[0], jnp.zeros_like(vals[0])
    for i in range(1, len(vals)):
        take = vals[i] > best
        best = jnp.where(take, vals[i], best)
        idx = jnp.where(take, float(i), idx)
    return idx, best


def _pick(idx, rows):
    out = rows[-1]
    for i in range(len(rows) - 2, -1, -1):
        out = jnp.where(idx == float(i), rows[i], out)
    return out


def _store_rows(ref, v):
    n = v.shape[0]
    for c in range(ROW_CHUNKS):
        ref[pl.ds(c, n, stride=ROW_CHUNKS), :] = v[:, c * LANES:(c + 1) * LANES]


def _load_rows(ref):
    n = ref.shape[0] // ROW_CHUNKS
    return jnp.concatenate(
        [ref[pl.ds(c, n, stride=ROW_CHUNKS), :] for c in range(ROW_CHUNKS)], axis=1)


def _route_rows(y, rw_ref, rbias_ref, tri_ref, run_sc):
    y_hi = y.astype(BF16)
    y_lo = (y - y_hi.astype(F32)).astype(BF16)
    prod = _dot(y_hi, rw_ref[...]) + _dot(y_lo, rw_ref[...])
    prod_t = prod.T
    logits = prod_t[:N_EXPERTS] + prod_t[N_EXPERTS:2 * N_EXPERTS]
    scores = 1.0 / (1.0 + jnp.exp(-logits))
    biased = scores + rbias_ref[...]
    b = [biased[e:e + 1] for e in range(N_EXPERTS)]
    sc = [scores[e:e + 1] for e in range(N_EXPERTS)]
    group_scores = [_top2_of_4(*b[g * 4:g * 4 + 4]) for g in range(N_GROUPS)]
    g_idx, _ = _argmax_first(group_scores)
    in_b = [_pick(g_idx, [b[g * 4 + j] for g in range(N_GROUPS)]) for j in range(EXPERTS_PER_GROUP)]
    in_s = [_pick(g_idx, [sc[g * 4 + j] for g in range(N_GROUPS)]) for j in range(EXPERTS_PER_GROUP)]
    l1, _ = _argmax_first(in_b)
    rest = [jnp.where(l1 == float(j), -jnp.inf, in_b[j]) for j in range(EXPERTS_PER_GROUP)]
    l2, _ = _argmax_first(rest)
    w1 = _pick(l1, in_s)
    w2 = _pick(l2, in_s)
    wsum = w1 + w2
    e1 = g_idx * float(EXPERTS_PER_GROUP) + l1
    e2 = g_idx * float(EXPERTS_PER_GROUP) + l2
    eid = lax.broadcasted_iota(jnp.int32, logits.shape, 0).astype(F32)
    hit1 = eid == e1
    hit2 = eid == e2
    member = jnp.where(jnp.logical_or(hit1, hit2), 1.0, 0.0)
    run = run_sc[...]
    before = _dot(member.astype(BF16), tri_ref[...]) + run[:, 0:1]
    r1 = jnp.sum(jnp.where(hit1, before, 0.0), axis=0, keepdims=True)
    r2 = jnp.sum(jnp.where(hit2, before, 0.0), axis=0, keepdims=True)
    run_sc[...] = run + jnp.sum(member, axis=1, keepdims=True)
    zero = jnp.zeros_like(e1)
    return jnp.concatenate([e1, e2, w1 / wsum, w2 / wsum, r1, r2, zero, zero], axis=0)


def _mixer_epilogue(x, h, g_ref, b_ref, rw_ref, rbias_ref, tri_ref, y_out, rt_out, cnt_out, run_sc):
    @pl.when(pl.program_id(0) == 0)
    def _():
        run_sc[...] = jnp.zeros(run_sc.shape, F32)

    y = _layer_norm(DEEPNORM_ALPHA * x + h, g_ref[...], b_ref[...])
    _store_rows(y_out, y)
    rt_out[...] = _route_rows(y, rw_ref, rbias_ref, tri_ref, run_sc)
    cnt_out[...] = run_sc[...]


def _epilogue_specs(ts):
    in_specs = [
        _const_spec((1, D_MODEL)),
        _const_spec((1, D_MODEL)),
        _const_spec((D_MODEL, LANES)),
        _const_spec((N_EXPERTS, 1)),
        _const_spec((ts, ts)),
    ]
    out_specs = [
        pl.BlockSpec((ROW_CHUNKS * ts, LANES), lambda i: (i, 0)),
        pl.BlockSpec((ROUTE_ROWS, ts), lambda i: (0, i)),
        _const_spec((N_EXPERTS, LANES)),
    ]
    return in_specs, out_specs


def _epilogue_shapes(s):
    return [
        jax.ShapeDtypeStruct((ROW_CHUNKS * s, LANES), F32),
        jax.ShapeDtypeStruct((ROUTE_ROWS, s), F32),
        jax.ShapeDtypeStruct((N_EXPERTS, LANES), F32),
    ]


_RUN_SCRATCH = pltpu.VMEM((N_EXPERTS, LANES), F32)


def _attn_out_kernel(x_ref, o_ref, wo_ref, *epilogue_refs):
    h = lax.dot_general(o_ref[...], wo_ref[...], (((0,), (0,)), ((), ())), preferred_element_type=F32)
    _mixer_epilogue(x_ref[...], h, *epilogue_refs)


def _attn_out(x, o, wo, g, b, rw, rbias, tri):
    s = x.shape[0]
    ts = min(TOKEN_TILE, s)
    ep_in, ep_out = _epilogue_specs(ts)
    return pl.pallas_call(
        _attn_out_kernel,
        grid=(s // ts,),
        in_specs=[
            pl.BlockSpec((ts, D_MODEL), lambda i: (i, 0)),
            pl.BlockSpec((N_HEADS * V_HEAD, ts), lambda i: (0, i)),
            _const_spec((N_HEADS * V_HEAD, D_MODEL)),
        ] + ep_in,
        out_specs=ep_out,
        out_shape=_epilogue_shapes(s),
        scratch_shapes=[_RUN_SCRATCH],
        compiler_params=_params("arbitrary"),
        name="attn_out_ln_route",
    )(x, o, wo, g, b, rw, rbias, tri)


def _conv_kernel(x_ref, win_ref, ck_ref, wout_ref, g_ref, b_ref, rw_ref, rbias_ref, tri_ref,
                 y_out, rt_out, cnt_out, u_sc, run_sc, *, ts):
    @pl.when(pl.program_id(0) == 0)
    def _():
        u_sc[0:8, :] = jnp.zeros((8, D_MODEL), F32)

    x = x_ref[...]
    xb = x.astype(BF16)
    c_gate = _dot(xb, win_ref[:, D_MODEL:2 * D_MODEL])
    u = c_gate * _dot(xb, win_ref[:, 2 * D_MODEL:])
    u_sc[8:8 + ts, :] = u
    ck = ck_ref[...]
    conv = u_sc[6:6 + ts, :] * ck[0:1] + u_sc[7:7 + ts, :] * ck[1:2] + u * ck[2:3]
    u_sc[0:8, :] = u_sc[ts:ts + 8, :]
    b_gate = _dot(xb, win_ref[:, :D_MODEL])
    h = _dot((b_gate * conv).astype(BF16), wout_ref[...])
    _mixer_epilogue(x, h, g_ref, b_ref, rw_ref, rbias_ref, tri_ref, y_out, rt_out, cnt_out, run_sc)


def _conv_mixer(x, win, ck, wout, g, b, rw, rbias, tri):
    s = x.shape[0]
    ts = min(TOKEN_TILE, s)
    ep_in, ep_out = _epilogue_specs(ts)
    return pl.pallas_call(
        functools.partial(_conv_kernel, ts=ts),
        grid=(s // ts,),
        in_specs=[
            pl.BlockSpec((ts, D_MODEL), lambda i: (i, 0)),
            _const_spec((D_MODEL, 3 * D_MODEL)),
            _const_spec((CONV_W, D_MODEL)),
            _const_spec((D_MODEL, D_MODEL)),
        ] + ep_in,
        out_specs=ep_out,
        out_shape=_epilogue_shapes(s),
        scratch_shapes=[pltpu.VMEM((ts + 8, D_MODEL), F32), _RUN_SCRATCH],
        compiler_params=_params("arbitrary"),
        name="conv_mixer_ln_route",
    )(x, win, ck, wout, g, b, rw, rbias, tri)


def _tile_plan(counts, tm, n_tiles_max):
    counts = counts[:, 0].astype(jnp.int32)
    tiles_per_expert = (counts + tm - 1) // tm
    tile_end = jnp.cumsum(tiles_per_expert)
    row_start = (tile_end - tiles_per_expert) * tm
    last_tile = jnp.maximum(tile_end - 1, 0)
    n_tiles = tile_end[-1:]
    tile = jnp.minimum(jnp.arange(n_tiles_max, dtype=jnp.int32), n_tiles - 1)
    tile_expert = jnp.sum((tile[:, None] >= tile_end[None, :]).astype(jnp.int32), axis=1)
    return row_start, last_tile, tile_expert, n_tiles


def _row(ref, r):
    return ref.at[pl.ds(pl.multiple_of(r * ROW_CHUNKS, ROW_CHUNKS), ROW_CHUNKS)]


def _dispatch_kernel(last_ref, nt_ref, pos_ref, y_ref, xs_ref, zero_sc, stage, sem, row_sems,
                     *, ts, tm, n_tiles_max):
    step = pl.program_id(0)
    slot = step % 2

    def tile(i):
        return xs_ref.at[pl.ds(pl.multiple_of(i * (ROW_CHUNKS * tm), ROW_CHUNKS * tm), ROW_CHUNKS * tm)]

    @pl.when(pl.program_id(0) == 0)
    def _():
        zero_sc[...] = jnp.zeros(zero_sc.shape, F32)
        for e in range(N_EXPERTS):
            pltpu.make_async_copy(zero_sc, tile(last_ref[e]), sem).start()
        for e in range(N_EXPERTS):
            pltpu.make_async_copy(zero_sc, tile(last_ref[e]), sem).wait()

        def fill(i, carry):
            copy = pltpu.make_async_copy(zero_sc, tile(i), sem)
            copy.start()
            copy.wait()
            return carry

        lax.fori_loop(nt_ref[0], n_tiles_max, fill, 0)

    stage[slot] = y_ref[...]

    def issue(t, carry):
        for k in range(2):
            pltpu.make_async_copy(_row(stage.at[slot], t), _row(xs_ref, pos_ref[0, k, t]),
                                  row_sems.at[slot]).start(priority=k)
        return carry

    lax.fori_loop(0, ts, issue, 0, unroll=8)

    def drain(which):
        for _ in range(2):
            pltpu.make_async_copy(stage.at[which], xs_ref.at[pl.ds(0, ROW_CHUNKS * ts)],
                                  row_sems.at[which]).wait()

    @pl.when(step > 0)
    def _():
        drain(1 - slot)

    @pl.when(step == pl.num_programs(0) - 1)
    def _():
        drain(slot)


def _pos_spec(ts):
    return pl.BlockSpec((1, 2, ts), lambda i, *_: (i, 0, 0), memory_space=pltpu.SMEM)


def _dispatch(last_tile, n_tiles, pos, y_rows, n_rows, tm):
    ts = pos.shape[-1]
    s = y_rows.shape[0] // ROW_CHUNKS
    return pl.pallas_call(
        functools.partial(_dispatch_kernel, ts=ts, tm=tm, n_tiles_max=n_rows // tm),
        grid_spec=pltpu.PrefetchScalarGridSpec(
            num_scalar_prefetch=2,
            grid=(s // ts,),
            in_specs=[
                _pos_spec(ts),
                pl.BlockSpec((ROW_CHUNKS * ts, LANES), lambda i, *_: (i, 0)),
            ],
            out_specs=pl.BlockSpec(memory_space=pl.ANY),
            scratch_shapes=[
                pltpu.VMEM((ROW_CHUNKS * tm, LANES), F32),
                pltpu.VMEM((2, ROW_CHUNKS * ts, LANES), F32),
                pltpu.SemaphoreType.DMA,
                pltpu.SemaphoreType.DMA((2,)),
            ],
        ),
        out_shape=jax.ShapeDtypeStruct((ROW_CHUNKS * n_rows, LANES), F32),
        compiler_params=_params("arbitrary"),
        name="moe_dispatch",
    )(last_tile, n_tiles, pos, y_rows)


def _expert_kernel(te_ref, nt_ref, xs_ref, wg_ref, wu_ref, wd_ref, ys_ref, wg_b, wu_b, wd_b):
    i = pl.program_id(0)

    @pl.when(i < nt_ref[0])
    def _():
        @pl.when(jnp.logical_or(i == 0, te_ref[i] != te_ref[jnp.maximum(i - 1, 0)]))
        def _():
            wg_b[...] = wg_ref[0, 0].astype(BF16)
            wu_b[...] = wu_ref[0, 0].astype(BF16)
            wd_b[...] = wd_ref[0, 0].astype(BF16)

        x = _load_rows(xs_ref).astype(BF16)
        hg = _dot(x, wg_b[...])
        hu = _dot(x, wu_b[...])
        hidden = (hg * (1.0 / (1.0 + jnp.exp(-hg)))) * hu
        _store_rows(ys_ref, _dot(hidden.astype(BF16), wd_b[...]))

    @pl.when(i >= nt_ref[0])
    def _():
        ys_ref[...] = jnp.zeros(ys_ref.shape, ys_ref.dtype)


def _experts(tile_expert, n_tiles, xs, w_gate, w_up, w_down, layer, tm):
    n_rows = xs.shape[0] // ROW_CHUNKS
    block = (ROW_CHUNKS * tm, LANES)
    weight = lambda i, te, nt: (layer, te[i], 0, 0)
    return pl.pallas_call(
        _expert_kernel,
        grid_spec=pltpu.PrefetchScalarGridSpec(
            num_scalar_prefetch=2,
            grid=(n_rows // tm,),
            in_specs=[
                pl.BlockSpec(block, lambda i, te, nt: (jnp.minimum(i, nt[0] - 1), 0)),
                pl.BlockSpec((1, 1, D_MODEL, D_EXPERT), weight),
                pl.BlockSpec((1, 1, D_MODEL, D_EXPERT), weight),
                pl.BlockSpec((1, 1, D_EXPERT, D_MODEL), weight),
            ],
            out_specs=pl.BlockSpec(block, lambda i, te, nt: (i, 0)),
            scratch_shapes=[
                pltpu.VMEM((D_MODEL, D_EXPERT), BF16),
                pltpu.VMEM((D_MODEL, D_EXPERT), BF16),
                pltpu.VMEM((D_EXPERT, D_MODEL), BF16),
            ],
        ),
        out_shape=jax.ShapeDtypeStruct(xs.shape, F32),
        compiler_params=_params("arbitrary"),
        name="moe_experts",
    )(tile_expert, n_tiles, xs, w_gate, w_up, w_down)


def _combine_kernel(pos_ref, pos_next_ref, y_ref, rt_ref, ys_ref, g_ref, b_ref, out_ref, got1, got2, sems,
                    *, ts):
    i = pl.program_id(0)
    slot = i % 2

    def gather(tile_pos_ref, into):
        def issue(t, carry):
            for k, got in enumerate((got1, got2)):
                pltpu.make_async_copy(_row(ys_ref, tile_pos_ref[0, k, t]), _row(got.at[into], t),
                                      sems.at[into]).start(priority=k)
            return carry

        lax.fori_loop(0, ts, issue, 0, unroll=8)

    @pl.when(i == 0)
    def _():
        gather(pos_ref, slot)

    @pl.when(i + 1 < pl.num_programs(0))
    def _():
        gather(pos_next_ref, 1 - slot)

    for got in (got1, got2):
        pltpu.make_async_copy(ys_ref.at[pl.ds(0, ROW_CHUNKS * ts)], got.at[slot], sems.at[slot]).wait()
    rt = rt_ref[...]
    moe = rt[:, 2:3] * _load_rows(got1.at[slot]) + rt[:, 3:4] * _load_rows(got2.at[slot])
    out_ref[...] = _layer_norm(DEEPNORM_ALPHA * _load_rows(y_ref) + moe, g_ref[...], b_ref[...])


def _combine(pos, y_rows, rt_cols, ys, g, b):
    ts = pos.shape[-1]
    s = y_rows.shape[0] // ROW_CHUNKS
    return pl.pallas_call(
        functools.partial(_combine_kernel, ts=ts),
        grid=(s // ts,),
        in_specs=[
            pl.BlockSpec((1, 2, ts), lambda i: (i, 0, 0), memory_space=pltpu.SMEM),
            pl.BlockSpec((1, 2, ts), lambda i: (jnp.minimum(i + 1, s // ts - 1), 0, 0),
                         memory_space=pltpu.SMEM),
            pl.BlockSpec((ROW_CHUNKS * ts, LANES), lambda i: (i, 0)),
            pl.BlockSpec((ts, ROUTE_ROWS), lambda i: (i, 0)),
            pl.BlockSpec(memory_space=pl.ANY),
            _const_spec((1, D_MODEL)),
            _const_spec((1, D_MODEL)),
        ],
        out_specs=pl.BlockSpec((ts, D_MODEL), lambda i: (i, 0)),
        out_shape=jax.ShapeDtypeStruct((s, D_MODEL), F32),
        scratch_shapes=[
            pltpu.VMEM((2, ROW_CHUNKS * ts, LANES), F32),
            pltpu.VMEM((2, ROW_CHUNKS * ts, LANES), F32),
            pltpu.SemaphoreType.DMA((2,)),
        ],
        compiler_params=_params("arbitrary"),
        name="moe_combine_ln",
    )(pos, pos, y_rows, rt_cols, ys, g, b)


def _moe(y_rows, rt, counts, w_gate, w_up, w_down, layer, g, b):
    s = rt.shape[1]
    ts = min(MOVE_TILE, s)
    tm = MOE_TILE
    n_rows = 2 * s + N_EXPERTS * tm
    row_start, last_tile, tile_expert, n_tiles = _tile_plan(counts, tm, n_rows // tm)
    expert = rt[0:2].astype(jnp.int32)
    first_row = sum(jnp.where(expert == e, row_start[e], 0) for e in range(N_EXPERTS))
    pos = (first_row + rt[4:6].astype(jnp.int32)).reshape(2, s // ts, ts).transpose(1, 0, 2)
    xs = _dispatch(last_tile, n_tiles, pos, y_rows, n_rows, tm)
    ys = _experts(tile_expert, n_tiles, xs, w_gate, w_up, w_down, layer, tm)
    return _combine(pos, y_rows, rt.T, ys, g, b)


def _rope_lane_pad(t):
    pad = [(0, 0)] * (t.ndim - 1)
    return jnp.pad(t, pad + [(QK_NOPE, LANES - QK_NOPE - QK_ROPE)])


def _rotate_half(t):
    return jnp.concatenate([-t[..., HALF_ROPE:], t[..., :HALF_ROPE]], axis=-1)


def _prep_mla(w_dqkv, w_uq, w_ukv, w_o):
    w_kr = w_dqkv[:, Q_LORA + KV_LORA:]
    wd = jnp.concatenate(
        [w_dqkv[:, :Q_LORA + KV_LORA], _rope_lane_pad(w_kr), _rope_lane_pad(_rotate_half(w_kr))],
        axis=1).astype(BF16)
    uq = w_uq.reshape(Q_LORA, N_HEADS, QK_NOPE + QK_ROPE)
    zq = jnp.zeros((Q_LORA, N_HEADS, LANES - QK_NOPE - QK_ROPE), F32)
    q_plain = jnp.concatenate([uq, zq], axis=-1)
    q_rot = jnp.concatenate(
        [jnp.zeros((Q_LORA, N_HEADS, QK_NOPE), F32), _rotate_half(uq[..., QK_NOPE:]), zq], axis=-1)
    wq = jnp.concatenate([q_plain, q_rot], axis=-1).reshape(Q_LORA, N_HEADS * 2 * LANES).astype(BF16)
    ukv = w_ukv.reshape(KV_LORA, N_HEADS, QK_NOPE + V_HEAD)
    wk = jnp.pad(ukv[..., :QK_NOPE], ((0, 0), (0, 0), (0, LANES - QK_NOPE)))
    wk = wk.reshape(KV_LORA, N_HEADS * LANES).astype(BF16)
    wvt = jnp.pad(ukv[..., QK_NOPE:], ((0, 0), (0, 0), (0, VT_ROWS - V_HEAD)))
    wvt = wvt.reshape(KV_LORA, N_HEADS * VT_ROWS).T.astype(BF16)
    wo = w_o.astype(BF16)
    return wd, wq, wk, wvt, wo


def _prep_router(router_w, router_bias):
    hi = router_w.astype(BF16)
    lo = (router_w - hi.astype(F32)).astype(BF16)
    rw = jnp.concatenate([hi, lo, jnp.zeros((D_MODEL, LANES - 2 * N_EXPERTS), BF16)], axis=1)
    return rw, router_bias.reshape(N_EXPERTS, 1).astype(F32)


def kernel(x, positions, mla_w_dqkv, mla_q_norm, mla_kv_norm, mla_w_uq, mla_w_ukv, mla_w_o,
           conv_w_in, conv_kernel, conv_w_out, router_w, router_bias,
           moe_w_gate, moe_w_up, moe_w_down, ln_mix_g, ln_mix_b, ln_ffn_g, ln_ffn_b):
    batch, s, _ = x.shape
    rw, rbias = _prep_router(router_w, router_bias)
    inv_freq = ROPE_THETA ** (-jnp.arange(0, QK_ROPE, 2, dtype=F32) / QK_ROPE)
    invf = _rope_lane_pad(jnp.concatenate([inv_freq, inv_freq]))[None, :]
    ts = min(TOKEN_TILE, s)
    tri = jnp.triu(jnp.ones((ts, ts), BF16), k=1)
    row = lambda t: t.reshape(1, -1).astype(F32)

    outs = []
    for bi in range(batch):
        xc = x[bi]
        pos_col = positions[bi].reshape(s, 1)
        for i in range(DEPTH):
            j = i // 2
            if i % 2 == 0:
                wd, wq, wk, wvt, wo = _prep_mla(mla_w_dqkv[j], mla_w_uq[j], mla_w_ukv[j], mla_w_o[j])
                q, k, vt = _mla_proj(xc, pos_col, invf, wd, row(mla_q_norm[j]), row(mla_kv_norm[j]),
                                     wq, wk, wvt)
                o = _attention(q, k, vt)
                y, rt, cnt = _attn_out(xc, o, wo, row(ln_mix_g[i]), row(ln_mix_b[i]), rw, rbias, tri)
            else:
                y, rt, cnt = _conv_mixer(xc, conv_w_in[j].astype(BF16), conv_kernel[j].astype(F32),
                                         conv_w_out[j].astype(BF16), row(ln_mix_g[i]),
                                         row(ln_mix_b[i]), rw, rbias, tri)
            xc = _moe(y, rt, cnt, moe_w_gate, moe_w_up, moe_w_down, i, row(ln_ffn_g[i]), row(ln_ffn_b[i]))
        outs.append(xc)
    return jnp.stack(outs, axis=0)
```
